```python
import jax, jax.numpy as jnp
from jax import lax
import numpy as np

D_MODEL = 1024
BATCH = 32
SEQ = 2048
DEPTH = 1
DEC_BATCH = 2
DEC_SEQ = 16384
PAST_LEN = 128

HEAD_DIM = 64
A_Q_HEADS = 8
A_KV_HEADS = 2
B_Q_HEADS = 8
B_KV_HEADS = 2
A_WIDTH = A_Q_HEADS * HEAD_DIM
B_WIDTH = B_Q_HEADS * HEAD_DIM
MIX_WIDTH = A_WIDTH + B_WIDTH
A_KV_WIDTH = A_KV_HEADS * HEAD_DIM
B_KV_WIDTH = B_KV_HEADS * HEAD_DIM
IN_COLS = A_WIDTH + 2 * A_KV_WIDTH + B_WIDTH + 2 * B_KV_WIDTH
IN_SPLITS = (A_WIDTH,
             A_WIDTH + A_KV_WIDTH,
             A_WIDTH + 2 * A_KV_WIDTH,
             2 * A_WIDTH + 2 * A_KV_WIDTH,
             2 * A_WIDTH + 2 * A_KV_WIDTH + B_KV_WIDTH)
WINDOW = 128
BLOCK = 128
ROPE_THETA = 10000.0
GRID_W = 64
N_MEM = 256
X_HEADS = 4
X_HEAD_DIM = 128
X_WIDTH = X_HEADS * X_HEAD_DIM
PEER_HEADS = 8
PEER_NKEYS = 128
PEER_EXPERTS = PEER_NKEYS * PEER_NKEYS
PEER_QDIM = 256
PEER_TOPK = 16
PEER_CHUNK = 128
EPS = 1e-6
NEG = -1e30

kernel_name = "hymba_peer_bidir_encoder"


def rms_norm(x, g):
    x32 = x.astype(jnp.float32)
    y = x32 * lax.rsqrt(jnp.mean(x32 * x32, axis=-1, keepdims=True) + EPS)
    return (y * g.astype(jnp.float32)).astype(x.dtype)


def rope(x, pos):
    d = x.shape[-1]
    inv = ROPE_THETA ** (-jnp.arange(0, d, 2, dtype=jnp.float32) / d)
    ang = pos.astype(jnp.float32)[:, None] * inv[None, :]
    cos = jnp.cos(ang)[:, None, :]
    sin = jnp.sin(ang)[:, None, :]
    x32 = x.astype(jnp.float32)
    x1, x2 = x32[..., : d // 2], x32[..., d // 2:]
    return jnp.concatenate([x1 * cos - x2 * sin, x2 * cos + x1 * sin], axis=-1).astype(x.dtype)


def axial_rope(x):
    S = x.shape[1]
    rows = S // GRID_W
    row = jnp.repeat(jnp.arange(rows, dtype=jnp.int32), GRID_W)
    col = jnp.tile(jnp.arange(GRID_W, dtype=jnp.int32), rows)
    h = x.shape[-1] // 2
    return jnp.concatenate([rope(x[..., :h], row), rope(x[..., h:], col)], axis=-1)


def window_sink_attention(q, k, v, sink):
    Bn, S, HQ, d = q.shape
    HKV = k.shape[2]
    G = HQ // HKV
    span = BLOCK + 2 * WINDOW
    pad = ((0, 0), (WINDOW, WINDOW), (0, 0), (0, 0))
    kp = jnp.pad(k, pad)
    vp = jnp.pad(v, pad)
    qg = q.reshape(Bn, S, HKV, G, d)
    sink_g = sink.astype(jnp.float32).reshape(HKV, G)[None, :, :, None]
    scale = d ** -0.5

    def one_block(i):
        start = i * BLOCK
        qb = lax.dynamic_slice_in_dim(qg, start, BLOCK, axis=1)
        kb = lax.dynamic_slice_in_dim(kp, start, span, axis=1)
        vb = lax.dynamic_slice_in_dim(vp, start, span, axis=1)
        qpos = start + jnp.arange(BLOCK)
        kpos = start - WINDOW + jnp.arange(span)
        valid = ((jnp.abs(qpos[:, None] - kpos[None, :]) <= WINDOW)
                 & (kpos >= 0)[None, :] & (kpos < S)[None, :])
        s = jnp.einsum('bqkgd,bjkd->bkgqj', qb, kb,
                       preferred_element_type=jnp.float32) * scale
        s = jnp.where(valid, s, NEG)
        m = jnp.maximum(s.max(axis=-1), sink_g)
        p = jnp.exp(s - m[..., None])
        den = p.sum(axis=-1) + jnp.exp(sink_g - m)
        p = p / den[..., None]
        o = jnp.einsum('bkgqj,bjkd->bqkgd', p, vb.astype(jnp.float32))
        return o.reshape(Bn, BLOCK, HQ * d).astype(v.dtype)

    out = lax.map(one_block, jnp.arange(S // BLOCK))
    return out.transpose(1, 0, 2, 3).reshape(Bn, S, HQ * d)


def dense_block_attention(q, k, v):
    Bn, S, HQ, d = q.shape
    HKV = k.shape[2]
    G = HQ // HKV
    qg = q.reshape(Bn, S, HKV, G, d)
    scale = d ** -0.5

    def one_block(i):
        qb = lax.dynamic_slice_in_dim(qg, i * BLOCK, BLOCK, axis=1)
        s = jnp.einsum('bqkgd,bjkd->bkgqj', qb, k,
                       preferred_element_type=jnp.float32) * scale
        p = jax.nn.softmax(s, axis=-1)
        o = jnp.einsum('bkgqj,bjkd->bqkgd', p, v.astype(jnp.float32))
        return o.reshape(Bn, BLOCK, HQ * d).astype(v.dtype)

    out = lax.map(one_block, jnp.arange(S // BLOCK))
    return out.transpose(1, 0, 2, 3).reshape(Bn, S, HQ * d)


def memory_cross_attention(h, mem_n, w_cq, w_ckv, cqn, ckn, w_co):
    Bn, S, _ = h.shape
    q = (h @ w_cq).reshape(Bn, S, X_HEADS, X_HEAD_DIM)
    k, v = jnp.split(mem_n @ w_ckv, 2, axis=-1)
    k = k.reshape(Bn, N_MEM, X_HEADS, X_HEAD_DIM)
    v = v.reshape(Bn, N_MEM, X_HEADS, X_HEAD_DIM)
    q = rms_norm(q, cqn)
    k = rms_norm(k, ckn)
    s = jnp.einsum('bqhd,bmhd->bhqm', q, k,
                   preferred_element_type=jnp.float32) * (X_HEAD_DIM ** -0.5)
    p = jax.nn.softmax(s, axis=-1)
    o = jnp.einsum('bhqm,bmhd->bqhd', p, v.astype(jnp.float32)).astype(h.dtype)
    return o.reshape(Bn, S, X_WIDTH) @ w_co


def peer_ffn(h, w_pq, pk1, pk2, peer_u, peer_v):
    Bn, S, D = h.shape
    T = Bn * S
    half = PEER_QDIM // 2
    hc = h.reshape(T // PEER_CHUNK, PEER_CHUNK, D)

    def one_chunk(xc):
        q = (xc @ w_pq).reshape(PEER_CHUNK, PEER_HEADS, PEER_QDIM)
        s1 = jnp.einsum('thd,hnd->thn', q[..., :half], pk1,
                        preferred_element_type=jnp.float32)
        s2 = jnp.einsum('thd,hnd->thn', q[..., half:], pk2,
                        preferred_element_type=jnp.float32)
        v1, i1 = lax.top_k(s1, PEER_TOPK)
        v2, i2 = lax.top_k(s2, PEER_TOPK)
        cand = (v1[..., :, None] + v2[..., None, :]).reshape(
            PEER_CHUNK, PEER_HEADS, PEER_TOPK * PEER_TOPK)
        sc, ic = lax.top_k(cand, PEER_TOPK)
        e = (jnp.take_along_axis(i1, ic // PEER_TOPK, axis=-1) * PEER_NKEYS
             + jnp.take_along_axis(i2, ic % PEER_TOPK, axis=-1))
        g = jax.nn.softmax(sc, axis=-1)
        u = peer_u[e]
        a = jax.nn.gelu(jnp.einsum('td,thkd->thk', xc, u,
                                   preferred_element_type=jnp.float32), approximate=False)
        w = (g * a).astype(xc.dtype)
        return jnp.einsum('thk,thkd->td', w, peer_v[e])

    return lax.map(one_chunk, hc).reshape(Bn, S, D)


def encoder_layer(x, mem, ln_mix, w_in, qn_a, kn_a, sink_a, qn_b, kn_b, go_a, go_b, w_out,
                  ln_x, ln_mem, w_cq, w_ckv, cqn, ckn, w_co,
                  ln_ff, w_pq, pk1, pk2, peer_u, peer_v):
    Bn, S, _ = x.shape
    h = rms_norm(x, ln_mix)
    z = h @ w_in
    qa, ka, va, qb, kb, vb = jnp.split(z, IN_SPLITS, axis=-1)
    qa = qa.reshape(Bn, S, A_Q_HEADS, HEAD_DIM)
    ka = ka.reshape(Bn, S, A_KV_HEADS, HEAD_DIM)
    va = va.reshape(Bn, S, A_KV_HEADS, HEAD_DIM)
    qb = qb.reshape(Bn, S, B_Q_HEADS, HEAD_DIM)
    kb = kb.reshape(Bn, S, B_KV_HEADS, HEAD_DIM)
    vb = vb.reshape(Bn, S, B_KV_HEADS, HEAD_DIM)
    t = jnp.arange(S, dtype=jnp.int32)
    qa = rope(rms_norm(qa, qn_a), t)
    ka = rope(rms_norm(ka, kn_a), t)
    qb = axial_rope(rms_norm(qb, qn_b))
    kb = axial_rope(rms_norm(kb, kn_b))
    o_a = rms_norm(window_sink_attention(qa, ka, va, sink_a), go_a)
    o_b = rms_norm(dense_block_attention(qb, kb, vb), go_b)
    x = x + jnp.concatenate([o_a, o_b], axis=-1) @ w_out
    x = x + memory_cross_attention(rms_norm(x, ln_x), rms_norm(mem, ln_mem),
                                   w_cq, w_ckv, cqn, ckn, w_co)
    x = x + peer_ffn(rms_norm(x, ln_ff), w_pq, pk1, pk2, peer_u, peer_v)
    return x


def setup_inputs(seed: int = 0) -> dict:
    key = jax.random.key(seed)
    ks = jax.random.split(key, 32)
    f32 = jnp.float32

    def nrm(k, shape, scale):
        return jax.random.normal(k, shape, f32) * scale

    def gain(k, shape):
        return 1.0 + 0.02 * jax.random.normal(k, shape, f32)

    L = DEPTH
    return {
        "x_prompt": nrm(ks[0], (BATCH, SEQ, D_MODEL), 1.0),
        "x_sample": nrm(ks[1], (DEC_BATCH, DEC_SEQ, D_MODEL), 1.0),
        "mem_prompt": nrm(ks[2], (BATCH, N_MEM, D_MODEL), 1.0),
        "mem_sample": nrm(ks[3], (DEC_BATCH, N_MEM, D_MODEL), 1.0),
        "ln_mix": gain(ks[4], (L, D_MODEL)),
        "w_in": nrm(ks[5], (L, D_MODEL, IN_COLS), D_MODEL ** -0.5),
        "qn_a": gain(ks[6], (L, HEAD_DIM)),
        "kn_a": gain(ks[7], (L, HEAD_DIM)),
        "sink_a": nrm(ks[8], (L, A_Q_HEADS), 0.5),
        "qn_b": gain(ks[9], (L, HEAD_DIM)),
        "kn_b": gain(ks[10], (L, HEAD_DIM)),
        "go_a": gain(ks[11], (L, A_WIDTH)),
        "go_b": gain(ks[12], (L, B_WIDTH)),
        "w_out": nrm(ks[13], (L, MIX_WIDTH, D_MODEL), MIX_WIDTH ** -0.5),
        "ln_x": gain(ks[14], (L, D_MODEL)),
        "ln_mem": gain(ks[15], (L, D_MODEL)),
        "w_cq": nrm(ks[16], (L, D_MODEL, X_WIDTH), D_MODEL ** -0.5),
        "w_ckv": nrm(ks[17], (L, D_MODEL, 2 * X_WIDTH), D_MODEL ** -0.5),
        "cqn": gain(ks[18], (L, X_HEAD_DIM)),
        "ckn": gain(ks[19], (L, X_HEAD_DIM)),
        "w_co": nrm(ks[20], (L, X_WIDTH, D_MODEL), X_WIDTH ** -0.5),
        "ln_ff": gain(ks[21], (L, D_MODEL)),
        "w_pq": nrm(ks[22], (L, D_MODEL, PEER_HEADS * PEER_QDIM), D_MODEL ** -0.5),
        "pk1": nrm(ks[23], (L, PEER_HEADS, PEER_NKEYS, PEER_QDIM // 2), (PEER_QDIM // 2) ** -0.5),
        "pk2": nrm(ks[24], (L, PEER_HEADS, PEER_NKEYS, PEER_QDIM // 2), (PEER_QDIM // 2) ** -0.5),
        "peer_u": nrm(ks[25], (L, PEER_EXPERTS, D_MODEL), D_MODEL ** -0.5),
        "peer_v": nrm(ks[26], (L, PEER_EXPERTS, D_MODEL), (PEER_HEADS * PEER_TOPK) ** -0.5),
    }


def reference(x_prompt, x_sample, mem_prompt, mem_sample,
              ln_mix, w_in, qn_a, kn_a, sink_a, qn_b, kn_b, go_a, go_b, w_out,
              ln_x, ln_mem, w_cq, w_ckv, cqn, ckn, w_co,
              ln_ff, w_pq, pk1, pk2, peer_u, peer_v):
    params = (ln_mix, w_in, qn_a, kn_a, sink_a, qn_b, kn_b, go_a, go_b, w_out,
              ln_x, ln_mem, w_cq, w_ckv, cqn, ckn, w_co,
              ln_ff, w_pq, pk1, pk2, peer_u, peer_v)
    y_prompt = x_prompt
    y_sample = x_sample
    for l in range(DEPTH):
        layer_params = [p[l] for p in params]
        y_prompt = encoder_layer(y_prompt, mem_prompt, *layer_params)
        y_sample = encoder_layer(y_sample, mem_sample, *layer_params)
    return (y_prompt, y_sample)
```

```python
import functools

import numpy as np
import jax
import jax.numpy as jnp
from jax import lax
from jax.experimental import pallas as pl
from jax.experimental.pallas import tpu as pltpu

F32 = jnp.float32
BF16 = jnp.bfloat16

D_MODEL = 1024
HEAD_DIM = 64
Q_HEADS = 8
KV_HEADS = 2
GROUP = Q_HEADS // KV_HEADS
Q_WIDTH = Q_HEADS * HEAD_DIM
KV_WIDTH = KV_HEADS * HEAD_DIM
IN_COLS = 2 * Q_WIDTH + 4 * KV_WIDTH
NORM_COLS = 2 * Q_WIDTH + 2 * KV_WIDTH
WINDOW = 128
ROPE_THETA = 10000.0
GRID_W = 64
N_MEM = 256
X_HEADS = 4
X_HEAD_DIM = 128
X_WIDTH = X_HEADS * X_HEAD_DIM
PEER_HEADS = 8
PEER_NKEYS = 128
PEER_EXPERTS = PEER_NKEYS * PEER_NKEYS
PEER_QDIM = 256
PEER_TOPK = 16
EPS = 1e-6
NEG = -1e30

LANES = 128
SUBLANES = 8
VMEM_LIMIT = 56 * 1024 * 1024


def _params(*sem):
    return pltpu.CompilerParams(dimension_semantics=sem, vmem_limit_bytes=VMEM_LIMIT)


def _rms(x, g):
    ms = jnp.mean(x * x, axis=-1, keepdims=True)
    return x * lax.rsqrt(ms + EPS) * g


def _dot(a, b):
    return jnp.dot(a, b, preferred_element_type=F32)


def _dot_nt(a, b):
    return lax.dot_general(a, b, (((1,), (1,)), ((), ())), preferred_element_type=F32)


def _inproj_kernel(x_ref, g_ref, w_ref, seg_ref, gain_ref, cosa_ref, sina_ref, cosb_ref, sinb_ref,
                   qa_ref, qb_ref, ka_ref, kb_ref, va_ref, vb_ref):
    h = _rms(x_ref[...], g_ref[...]).astype(BF16)
    z = _dot(h, w_ref[...])
    lane = lax.broadcasted_iota(jnp.int32, (1, LANES), 1)
    seg = seg_ref[...]

    def norm_rope(c, cos, sin, first, half):
        zc = z[:, c * LANES:(c + 1) * LANES]
        sq = zc * zc
        hi = sq.astype(BF16)
        lo = (sq - hi.astype(F32)).astype(BF16)
        ms = (_dot(hi, seg) + _dot(lo, seg)) * (1.0 / HEAD_DIM)
        y = zc * lax.rsqrt(ms + EPS) * gain_ref[:, c * LANES:(c + 1) * LANES]
        rot = jnp.where(first, pltpu.roll(y, LANES - half, 1), pltpu.roll(y, half, 1))
        return y * cos + rot * sin

    first_a = (lane % HEAD_DIM) < (HEAD_DIM // 2)
    first_b = (lane % (HEAD_DIM // 2)) < (HEAD_DIM // 4)
    cosa, sina, cosb, sinb = cosa_ref[...], sina_ref[...], cosb_ref[...], sinb_ref[...]
    nq = Q_WIDTH // LANES
    for c in range(nq):
        qa_ref[:, c * LANES:(c + 1) * LANES] = norm_rope(c, cosa, sina, first_a, HEAD_DIM // 2).astype(BF16)
        qb_ref[:, c * LANES:(c + 1) * LANES] = norm_rope(nq + c, cosb, sinb, first_b, HEAD_DIM // 4).astype(BF16)
    ka_ref[...] = norm_rope(2 * nq, cosa, sina, first_a, HEAD_DIM // 2).astype(BF16)
    kb_ref[...] = norm_rope(2 * nq + 1, cosb, sinb, first_b, HEAD_DIM // 4).astype(BF16)
    va_ref[...] = z[:, NORM_COLS:NORM_COLS + KV_WIDTH].astype(BF16)
    vb_ref[...] = z[:, NORM_COLS + KV_WIDTH:].astype(BF16)


def _rope_tables(seq):
    lane = np.arange(LANES)
    t = jnp.arange(seq, dtype=jnp.int32)
    inv_a = ROPE_THETA ** (-jnp.arange(0, HEAD_DIM, 2, dtype=F32) / HEAD_DIM)
    ang_a = t.astype(F32)[:, None] * inv_a[None, :]
    idx_a = lane % (HEAD_DIM // 2)
    sign_a = np.where((lane % HEAD_DIM) < HEAD_DIM // 2, -1.0, 1.0).astype(np.float32)
    cos_a = jnp.cos(ang_a)[:, idx_a]
    sin_a = jnp.sin(ang_a)[:, idx_a] * sign_a[None, :]
    hd2 = HEAD_DIM // 2
    inv_b = ROPE_THETA ** (-jnp.arange(0, hd2, 2, dtype=F32) / hd2)
    row = (t // GRID_W).astype(F32)
    col = (t % GRID_W).astype(F32)
    ang_row = row[:, None] * inv_b[None, :]
    ang_col = col[:, None] * inv_b[None, :]
    idx_b = lane % (hd2 // 2)
    is_row = ((lane % HEAD_DIM) < hd2)[None, :]
    sign_b = np.where((lane % hd2) < hd2 // 2, -1.0, 1.0).astype(np.float32)
    cos_b = jnp.where(is_row, jnp.cos(ang_row)[:, idx_b], jnp.cos(ang_col)[:, idx_b])
    sin_b = jnp.where(is_row, jnp.sin(ang_row)[:, idx_b], jnp.sin(ang_col)[:, idx_b]) * sign_b[None, :]
    return cos_a, sin_a, cos_b, sin_b


def _inproj(x2d, seq, ln_mix, w_in_p, seg, gain, tables, tm):
    t_total = x2d.shape[0]
    nblk_seq = seq // tm
    tok = lambda w: pl.BlockSpec((tm, w), lambda i: (i, 0))
    const = lambda a: pl.BlockSpec(a.shape, lambda i: (0,) * a.ndim)
    tab = pl.BlockSpec((tm, LANES), lambda i: (i % nblk_seq, 0))
    outs = [jax.ShapeDtypeStruct((t_total, w), BF16) for w in
            (Q_WIDTH, Q_WIDTH, KV_WIDTH, KV_WIDTH, KV_WIDTH, KV_WIDTH)]
    return pl.pallas_call(
        _inproj_kernel,
        out_shape=outs,
        grid=(t_total // tm,),
        in_specs=[tok(D_MODEL), const(ln_mix), const(w_in_p), const(seg), const(gain), tab, tab, tab, tab],
        out_specs=[tok(Q_WIDTH), tok(Q_WIDTH), tok(KV_WIDTH), tok(KV_WIDTH), tok(KV_WIDTH), tok(KV_WIDTH)],
        compiler_params=_params("parallel"),
        name="inproj",
    )(x2d, ln_mix, w_in_p, seg, gain, *tables)


def _attn_a_kernel(q_ref, k_ref, v_ref, sink_ref, go_ref, o_ref, *, seq):
    bq = WINDOW
    span = bq + 2 * WINDOW
    qi = pl.program_id(1)
    start = qi * bq
    c = jnp.clip(start - WINDOW, 0, seq - span)
    c = pl.multiple_of(c, bq)
    q = q_ref[0]
    k = k_ref[0, pl.ds(c, span), :]
    v = v_ref[0, pl.ds(c, span), :]
    qpos = start + lax.broadcasted_iota(jnp.int32, (bq, 1), 0)
    kpos = c + lax.broadcasted_iota(jnp.int32, (1, span), 1)
    valid = jnp.abs(qpos - kpos) <= WINDOW
    outs = []
    for h in range(Q_HEADS):
        g = h // GROUP
        qh = q[:, h * HEAD_DIM:(h + 1) * HEAD_DIM]
        kg = k[:, g * HEAD_DIM:(g + 1) * HEAD_DIM]
        vg = v[:, g * HEAD_DIM:(g + 1) * HEAD_DIM]
        s = jnp.where(valid, _dot_nt(qh, kg), NEG)
        sink = sink_ref[:, h:h + 1]
        m = jnp.maximum(jnp.max(s, axis=-1, keepdims=True), sink)
        p = jnp.exp(s - m)
        den = jnp.sum(p, axis=-1, keepdims=True) + jnp.exp(sink - m)
        outs.append(_dot(p.astype(BF16), vg) / den)
    o = jnp.concatenate(outs, axis=-1)
    o_ref[0] = _rms(o, go_ref[...]).astype(BF16)


def _attn_a(q, k, v, sink, go):
    b, seq, _ = q.shape
    assert seq % WINDOW == 0 and seq >= 3 * WINDOW
    return pl.pallas_call(
        functools.partial(_attn_a_kernel, seq=seq),
        out_shape=jax.ShapeDtypeStruct((b, seq, Q_WIDTH), BF16),
        grid=(b, seq // WINDOW),
        in_specs=[pl.BlockSpec((1, WINDOW, Q_WIDTH), lambda i, j: (i, j, 0)),
                  pl.BlockSpec((1, seq, KV_WIDTH), lambda i, j: (i, 0, 0)),
                  pl.BlockSpec((1, seq, KV_WIDTH), lambda i, j: (i, 0, 0)),
                  pl.BlockSpec(sink.shape, lambda i, j: (0, 0)),
                  pl.BlockSpec(go.shape, lambda i, j: (0, 0))],
        out_specs=pl.BlockSpec((1, WINDOW, Q_WIDTH), lambda i, j: (i, j, 0)),
        compiler_params=_params("parallel", "arbitrary"),
        name="attn_a",
    )(q, k, v, sink, go)


def _attn_b_kernel(q_ref, k_ref, v_ref, go_ref, o_ref, *, seq, tq, tk):
    q = q_ref[0]
    outs = []
    for g in range(KV_HEADS):
        q4 = jnp.concatenate(
            [q[:, (g * GROUP + j) * HEAD_DIM:(g * GROUP + j + 1) * HEAD_DIM] for j in range(GROUP)], axis=0)

        def body(kb, carry, g=g, q4=q4):
            m, l, acc = carry
            off = pl.multiple_of(kb * tk, tk)
            k = k_ref[0, pl.ds(off, tk), g * HEAD_DIM:(g + 1) * HEAD_DIM]
            v = v_ref[0, pl.ds(off, tk), g * HEAD_DIM:(g + 1) * HEAD_DIM]
            s = _dot_nt(q4, k)
            m_new = jnp.maximum(m, jnp.max(s, axis=-1, keepdims=True))
            alpha = jnp.exp(m - m_new)
            p = jnp.exp(s - m_new)
            l = alpha * l + jnp.sum(p, axis=-1, keepdims=True)
            acc = alpha * acc + _dot(p.astype(BF16), v)
            return m_new, l, acc

        init = (jnp.full((GROUP * tq, 1), NEG, F32), jnp.zeros((GROUP * tq, 1), F32),
                jnp.zeros((GROUP * tq, HEAD_DIM), F32))
        _, l, acc = lax.fori_loop(0, seq // tk, body, init)
        og = acc / l
        outs += [og[j * tq:(j + 1) * tq] for j in range(GROUP)]
    o = jnp.concatenate(outs, axis=-1)
    o_ref[0] = _rms(o, go_ref[...]).astype(BF16)


def _attn_b(q, k, v, go, tq, tk):
    b, seq, _ = q.shape
    return pl.pallas_call(
        functools.partial(_attn_b_kernel, seq=seq, tq=tq, tk=tk),
        out_shape=jax.ShapeDtypeStruct((b, seq, Q_WIDTH), BF16),
        grid=(b, seq // tq),
        in_specs=[pl.BlockSpec((1, tq, Q_WIDTH), lambda i, j: (i, j, 0)),
                  pl.BlockSpec((1, seq, KV_WIDTH), lambda i, j: (i, 0, 0)),
                  pl.BlockSpec((1, seq, KV_WIDTH), lambda i, j: (i, 0, 0)),
                  pl.BlockSpec(go.shape, lambda i, j: (0, 0))],
        out_specs=pl.BlockSpec((1, tq, Q_WIDTH), lambda i, j: (i, j, 0)),
        compiler_params=_params("parallel", "arbitrary"),
        name="attn_b",
    )(q, k, v, go)


def _memkv_kernel(mem_ref, g_ref, w_ref, ckn_ref, k_ref, v_ref):
    h = _rms(mem_ref[0], g_ref[...]).astype(BF16)
    kv = _dot(h, w_ref[...])
    ckn = ckn_ref[...]
    for hd in range(X_HEADS):
        sl = slice(hd * X_HEAD_DIM, (hd + 1) * X_HEAD_DIM)
        k_ref[0, :, sl] = _rms(kv[:, sl], ckn).astype(BF16)
    v_ref[0] = kv[:, X_WIDTH:].astype(BF16)


def _memkv(mem, ln_mem, w_ckv, ckn):
    b = mem.shape[0]
    const = lambda a: pl.BlockSpec(a.shape, lambda i: (0,) * a.ndim)
    return pl.pallas_call(
        _memkv_kernel,
        out_shape=[jax.ShapeDtypeStruct((b, N_MEM, X_WIDTH), BF16)] * 2,
        grid=(b,),
        in_specs=[pl.BlockSpec((1, N_MEM, D_MODEL), lambda i: (i, 0, 0)), const(ln_mem), const(w_ckv), const(ckn)],
        out_specs=[pl.BlockSpec((1, N_MEM, X_WIDTH), lambda i: (i, 0, 0))] * 2,
        compiler_params=_params("parallel"),
        name="memkv",
    )(mem, ln_mem, w_ckv, ckn)


def _mid_kernel(x_ref, oa_ref, ob_ref, woa_ref, wob_ref, lnx_ref, wcq_ref, cqn_ref, kc_ref, vc_ref, wco_ref,
                o_ref):
    x1 = x_ref[...] + _dot(oa_ref[...], woa_ref[...]) + _dot(ob_ref[...], wob_ref[...])
    h = _rms(x1, lnx_ref[...]).astype(BF16)
    qc = _dot(h, wcq_ref[...])
    cqn = cqn_ref[...]
    outs = []
    for hd in range(X_HEADS):
        sl = slice(hd * X_HEAD_DIM, (hd + 1) * X_HEAD_DIM)
        qn = _rms(qc[:, sl], cqn).astype(BF16)
        s = _dot_nt(qn, kc_ref[0, :, sl])
        m = jnp.max(s, axis=-1, keepdims=True)
        p = jnp.exp(s - m)
        den = jnp.sum(p, axis=-1, keepdims=True)
        outs.append((_dot(p.astype(BF16), vc_ref[0, :, sl]) / den).astype(BF16))
    o = jnp.concatenate(outs, axis=-1)
    o_ref[...] = x1 + _dot(o, wco_ref[...])


def _mid(x2d, seq, oa, ob, w_out_a, w_out_b, ln_x, w_cq, cqn_s, kc, vc, w_co, tm):
    t_total = x2d.shape[0]
    nblk_seq = seq // tm
    tok = lambda w: pl.BlockSpec((tm, w), lambda i: (i, 0))
    const = lambda a: pl.BlockSpec(a.shape, lambda i: (0,) * a.ndim)
    memspec = pl.BlockSpec((1, N_MEM, X_WIDTH), lambda i: (i // nblk_seq, 0, 0))
    return pl.pallas_call(
        _mid_kernel,
        out_shape=jax.ShapeDtypeStruct((t_total, D_MODEL), F32),
        grid=(t_total // tm,),
        in_specs=[tok(D_MODEL), tok(Q_WIDTH), tok(Q_WIDTH), const(w_out_a), const(w_out_b), const(ln_x),
                  const(w_cq), const(cqn_s), memspec, memspec, const(w_co)],
        out_specs=tok(D_MODEL),
        compiler_params=_params("parallel"),
        name="mid",
    )(x2d, oa, ob, w_out_a, w_out_b, ln_x, w_cq, cqn_s, kc, vc, w_co)


def _oddeven_merge(lo, hi, r):
    step = r * 2
    if step < hi - lo:
        yield from _oddeven_merge(lo, hi, step)
        yield from _oddeven_merge(lo + r, hi, step)
        yield from [(i, i + r) for i in range(lo + r, hi - r, step)]
    else:
        yield (lo, lo + r)


def _oddeven_sort_pairs(lo, hi):
    if hi - lo >= 1:
        mid = lo + (hi - lo) // 2
        yield from _oddeven_sort_pairs(lo, mid)
        yield from _oddeven_sort_pairs(mid + 1, hi)
        yield from _oddeven_merge(lo, hi, 1)


_SORT16 = tuple(_oddeven_sort_pairs(0, PEER_TOPK - 1))
_BITONIC16 = tuple((i, i + d) for d in (8, 4, 2, 1) for i in range(PEER_TOPK) if not i & d)


def _apply_network(xs, pairs):
    xs = list(xs)
    for i, j in pairs:
        xs[i], xs[j] = jnp.maximum(xs[i], xs[j]), jnp.minimum(xs[i], xs[j])
    return xs


def _top_half(xs, ys):
    n = len(xs)
    return [jnp.maximum(xs[a], ys[n - 1 - a]) for a in range(n)]


def _top16_sorted(x):
    rows = [x[a * SUBLANES:(a + 1) * SUBLANES, :] for a in range(PEER_NKEYS // SUBLANES)]
    rows = _apply_network(rows, _SORT16)
    for shift in (4, 2, 1):
        other = [pltpu.roll(r, shift, 0) for r in rows]
        rows = _apply_network(_top_half(rows, other), _BITONIC16)
    return rows


def _top16_products(e1, e2):
    k = PEER_TOPK
    row = lambda r: [e1[r] * e2[q] for q in range(k // (r + 1))]
    a = row(0)
    col = [e1[r] * e2[0] for r in range(8, 16)]
    b = _apply_network(row(1) + col[::-1], _BITONIC16)
    c = _apply_network(row(2) + row(3) + row(4) + row(5) + row(6), _SORT16)
    d = row(7)
    t = _apply_network(_top_half(a, b), _BITONIC16)
    t = _apply_network(_top_half(t, c), _BITONIC16)
    t[k - 1] = jnp.maximum(t[k - 1], d[0])
    t[k - 2] = jnp.maximum(t[k - 2], d[1])
    return t


def _tree(op, xs):
    xs = list(xs)
    while len(xs) > 1:
        xs = [op(xs[i], xs[i + 1]) for i in range(0, len(xs) - 1, 2)] + ([xs[-1]] if len(xs) % 2 else [])
    return xs[0]


def _pprep_kernel(x_ref, g_ref, wpq_ref, pk1_ref, pk2_ref, h_ref, e1_ref, e2_ref, thr_ref, p1_ref, p2_ref, rz_ref,
                  *, tm):
    h = _rms(x_ref[...], g_ref[...]).astype(BF16)
    h_ref[...] = h
    half = PEER_QDIM // 2
    ngrp = tm // LANES

    def head(hd, carry):
        q = _dot(h, wpq_ref[hd]).astype(BF16)
        s1 = _dot_nt(pk1_ref[hd], q[:, :half])
        s2 = _dot_nt(pk2_ref[hd], q[:, half:])
        e1 = jnp.exp(s1 - jnp.max(s1, axis=0, keepdims=True))
        e2 = jnp.exp(s2 - jnp.max(s2, axis=0, keepdims=True))
        r0 = pl.multiple_of(hd * PEER_NKEYS, PEER_NKEYS)
        e1_ref[pl.ds(r0, PEER_NKEYS), :] = e1
        e2_ref[pl.ds(r0, PEER_NKEYS), :] = e2
        for lg in range(ngrp):
            ls = slice(lg * LANES, (lg + 1) * LANES)
            t1 = _top16_sorted(e1[:, ls])
            t2 = _top16_sorted(e2[:, ls])
            for r in range(PEER_TOPK):
                p1_ref[hd, r, :, ls] = t1[r]
                p2_ref[hd, r, :, ls] = t2[r]
        return carry

    lax.fori_loop(0, PEER_HEADS, head, 0)

    sub = lax.broadcasted_iota(jnp.int32, (SUBLANES, LANES), 0)

    def packed(ref, r, ls):
        out = ref[0, r, :, ls]
        for hd in range(1, PEER_HEADS):
            out = jnp.where(sub == hd, ref[hd, r, :, ls], out)
        return out

    for lg in range(ngrp):
        ls = slice(lg * LANES, (lg + 1) * LANES)
        t1 = [packed(p1_ref, r, ls) for r in range(PEER_TOPK)]
        t2 = [packed(p2_ref, r, ls) for r in range(PEER_TOPK)]
        rz = 1.0 / _tree(jnp.add, _top16_products(t1, t2))
        t1n = [a * rz for a in t1]
        thr_ref[:, ls] = _tree(jnp.minimum, _top16_products(t1n, t2))
        rz_ref[:, ls] = rz

    for hd in range(PEER_HEADS):
        rs = slice(hd * PEER_NKEYS, (hd + 1) * PEER_NKEYS)
        e1_ref[rs, :] = e1_ref[rs, :] * rz_ref[hd:hd + 1, :]


def _pprep(x2, ln_ff, wpq_h, pk1, pk2, tm):
    t_total = x2.shape[0]
    const = lambda a: pl.BlockSpec(a.shape, lambda i: (0,) * a.ndim)
    hk = PEER_HEADS * PEER_NKEYS
    return pl.pallas_call(
        functools.partial(_pprep_kernel, tm=tm),
        out_shape=[jax.ShapeDtypeStruct((t_total, D_MODEL), BF16),
                   jax.ShapeDtypeStruct((hk, t_total), F32),
                   jax.ShapeDtypeStruct((hk, t_total), F32),
                   jax.ShapeDtypeStruct((PEER_HEADS, t_total), F32)],
        grid=(t_total // tm,),
        in_specs=[pl.BlockSpec((tm, D_MODEL), lambda i: (i, 0)), const(ln_ff), const(wpq_h), const(pk1), const(pk2)],
        out_specs=[pl.BlockSpec((tm, D_MODEL), lambda i: (i, 0)),
                   pl.BlockSpec((hk, tm), lambda i: (0, i)),
                   pl.BlockSpec((hk, tm), lambda i: (0, i)),
                   pl.BlockSpec((PEER_HEADS, tm), lambda i: (0, i))],
        scratch_shapes=[pltpu.VMEM((PEER_HEADS, PEER_TOPK, SUBLANES, tm), F32),
                        pltpu.VMEM((PEER_HEADS, PEER_TOPK, SUBLANES, tm), F32),
                        pltpu.VMEM((PEER_HEADS, tm), F32)],
        compiler_params=_params("parallel"),
        name="pprep",
    )(x2, ln_ff, wpq_h, pk1, pk2)


def _pdense_kernel(x_ref, h_ref, e1_ref, e2_ref, thr_ref, u_ref, vt_ref, o_ref, acc_ref, wg_ref, *, rows):
    e = pl.program_id(1)

    @pl.when(e == 0)
    def _():
        acc_ref[...] = jnp.zeros_like(acc_ref)

    for r in range(rows):
        rs = slice(r * PEER_NKEYS, (r + 1) * PEER_NKEYS)
        a = _dot_nt(u_ref[rs, :], h_ref[...])
        act = 0.5 * a * (1.0 + lax.erf(a * (2.0 ** -0.5)))
        i1 = e * rows + r
        w = None
        for hd in range(PEER_HEADS):
            p = e1_ref[pl.ds(hd * PEER_NKEYS + i1, 1), :] * e2_ref[hd * PEER_NKEYS:(hd + 1) * PEER_NKEYS, :]
            sel = jnp.where(p >= thr_ref[hd:hd + 1, :], p, 0.0)
            w = sel if w is None else w + sel
        wg_ref[rs, :] = (w * act).astype(BF16)
    acc_ref[...] += _dot(vt_ref[...], wg_ref[...])

    @pl.when(e == pl.num_programs(1) - 1)
    def _():
        o_ref[...] = x_ref[...] + acc_ref[...].T


def _pdense(x2, h3, e1n, e2, thr, u, vt, tm, rows):
    t_total = x2.shape[0]
    en = rows * PEER_NKEYS
    hk = PEER_HEADS * PEER_NKEYS
    return pl.pallas_call(
        functools.partial(_pdense_kernel, rows=rows),
        out_shape=jax.ShapeDtypeStruct((t_total, D_MODEL), F32),
        grid=(t_total // tm, PEER_EXPERTS // en),
        in_specs=[pl.BlockSpec((tm, D_MODEL), lambda i, j: (i, 0)),
                  pl.BlockSpec((tm, D_MODEL), lambda i, j: (i, 0)),
                  pl.BlockSpec((hk, tm), lambda i, j: (0, i)),
                  pl.BlockSpec((hk, tm), lambda i, j: (0, i)),
                  pl.BlockSpec((PEER_HEADS, tm), lambda i, j: (0, i)),
                  pl.BlockSpec((en, D_MODEL), lambda i, j: (j, 0)),
                  pl.BlockSpec((D_MODEL, en), lambda i, j: (0, j))],
        out_specs=pl.BlockSpec((tm, D_MODEL), lambda i, j: (i, 0)),
        scratch_shapes=[pltpu.VMEM((D_MODEL, tm), F32), pltpu.VMEM((en, tm), BF16)],
        compiler_params=_params("parallel", "arbitrary"),
        name="pdense",
    )(x2, h3, e1n, e2, thr, u, vt)


def _prepare_weights(ln_mix, w_in, qn_a, kn_a, sink_a, qn_b, kn_b, go_a, go_b, w_out, ln_x, ln_mem, w_cq, w_ckv,
                     cqn, ckn, w_co, ln_ff, w_pq, pk1, pk2, peer_u, peer_v):
    row = lambda a: a.reshape(1, -1).astype(F32)
    qa0, ka0, va0 = 0, Q_WIDTH, Q_WIDTH + KV_WIDTH
    qb0 = Q_WIDTH + 2 * KV_WIDTH
    kb0, vb0 = qb0 + Q_WIDTH, qb0 + Q_WIDTH + KV_WIDTH
    cols = lambda s, w: w_in[:, s:s + w]
    w_in_p = jnp.concatenate([cols(qa0, Q_WIDTH), cols(qb0, Q_WIDTH), cols(ka0, KV_WIDTH), cols(kb0, KV_WIDTH),
                              cols(va0, KV_WIDTH), cols(vb0, KV_WIDTH)], axis=1).astype(BF16)
    scale = HEAD_DIM ** -0.5
    gain = jnp.concatenate([jnp.tile(qn_a, Q_HEADS) * scale, jnp.tile(qn_b, Q_HEADS) * scale,
                            jnp.tile(kn_a, KV_HEADS), jnp.tile(kn_b, KV_HEADS)]).reshape(1, NORM_COLS).astype(F32)
    lane = np.arange(LANES)
    seg = jnp.asarray((lane[:, None] // HEAD_DIM) == (lane[None, :] // HEAD_DIM), dtype=BF16)
    return dict(
        ln_mix=row(ln_mix), w_in_p=w_in_p, gain=gain, seg=seg, sink=row(sink_a), go_a=row(go_a), go_b=row(go_b),
        w_out_a=w_out[:Q_WIDTH].astype(BF16), w_out_b=w_out[Q_WIDTH:].astype(BF16),
        ln_x=row(ln_x), ln_mem=row(ln_mem), w_cq=w_cq.astype(BF16), w_ckv=w_ckv.astype(BF16),
        cqn_s=row(cqn) * (X_HEAD_DIM ** -0.5), ckn=row(ckn), w_co=w_co.astype(BF16), ln_ff=row(ln_ff),
        wpq_h=w_pq.reshape(D_MODEL, PEER_HEADS, PEER_QDIM).transpose(1, 0, 2).astype(BF16),
        pk1=pk1.astype(BF16), pk2=pk2.astype(BF16), u=peer_u.astype(BF16), vt=peer_v.T.astype(BF16))


def _layer(x, mem, p, *, tm_in=512, tq_b=256, tk_b=512, tm_mid=512, tm_prep=256, tm_dense=512, rows=4):
    b, seq, _ = x.shape
    x2d = x.reshape(b * seq, D_MODEL)
    tables = _rope_tables(seq)
    qa, qb, ka, kb, va, vb = _inproj(x2d, seq, p["ln_mix"], p["w_in_p"], p["seg"], p["gain"], tables, tm_in)
    r3 = lambda a: a.reshape(b, seq, a.shape[-1])
    oa = _attn_a(r3(qa), r3(ka), r3(va), p["sink"], p["go_a"])
    ob = _attn_b(r3(qb), r3(kb), r3(vb), p["go_b"], tq_b, tk_b)
    kc, vc = _memkv(mem, p["ln_mem"], p["w_ckv"], p["ckn"])
    x2 = _mid(x2d, seq, oa.reshape(b * seq, Q_WIDTH), ob.reshape(b * seq, Q_WIDTH), p["w_out_a"], p["w_out_b"],
              p["ln_x"], p["w_cq"], p["cqn_s"], kc, vc, p["w_co"], tm_mid)
    h3, e1n, e2, thr = _pprep(x2, p["ln_ff"], p["wpq_h"], p["pk1"], p["pk2"], tm_prep)
    y = _pdense(x2, h3, e1n, e2, thr, p["u"], p["vt"], tm_dense, rows)
    return y.reshape(b, seq, D_MODEL)


def kernel(x_prompt, x_sample, mem_prompt, mem_sample, ln_mix, w_in, qn_a, kn_a, sink_a, qn_b, kn_b, go_a, go_b,
           w_out, ln_x, ln_mem, w_cq, w_ckv, cqn, ckn, w_co, ln_ff, w_pq, pk1, pk2, peer_u, peer_v):
    params = (ln_mix, w_in, qn_a, kn_a, sink_a, qn_b, kn_b, go_a, go_b, w_out, ln_x, ln_mem, w_cq, w_ckv,
              cqn, ckn, w_co, ln_ff, w_pq, pk1, pk2, peer_u, peer_v)
    y_prompt, y_sample = x_prompt, x_sample
    for l in range(ln_mix.shape[0]):
        p = _prepare_weights(*[a[l] for a in params])
        y_prompt = _layer(y_prompt, mem_prompt, p)
        y_sample = _layer(y_sample, mem_sample, p)
    return (y_prompt, y_sample)
```

```python
import functools

import numpy as np
import jax
import jax.numpy as jnp
from jax import lax
from jax.experimental import pallas as pl
from jax.experimental.pallas import tpu as pltpu

F32 = jnp.float32
BF16 = jnp.bfloat16

D_MODEL = 1024
HEAD_DIM = 64
Q_HEADS = 8
KV_HEADS = 2
GROUP = Q_HEADS // KV_HEADS
Q_WIDTH = Q_HEADS * HEAD_DIM
KV_WIDTH = KV_HEADS * HEAD_DIM
IN_COLS = 2 * Q_WIDTH + 4 * KV_WIDTH
NORM_COLS = 2 * Q_WIDTH + 2 * KV_WIDTH
WINDOW = 128
ROPE_THETA = 10000.0
GRID_W = 64
N_MEM = 256
X_HEADS = 4
X_HEAD_DIM = 128
X_WIDTH = X_HEADS * X_HEAD_DIM
PEER_HEADS = 8
PEER_NKEYS = 128
PEER_EXPERTS = PEER_NKEYS * PEER_NKEYS
PEER_QDIM = 256
PEER_TOPK = 16
EPS = 1e-6
NEG = -1e30

LANES = 128
SUBLANES = 8
VMEM_LIMIT = 56 * 1024 * 1024


def _params(*sem):
    return pltpu.CompilerParams(dimension_semantics=sem, vmem_limit_bytes=VMEM_LIMIT)


def _rms(x, g):
    ms = jnp.mean(x * x, axis=-1, keepdims=True)
    return x * lax.rsqrt(ms + EPS) * g


def _dot(a, b):
    return jnp.dot(a, b, preferred_element_type=F32)


def _dot_nt(a, b):
    return lax.dot_general(a, b, (((1,), (1,)), ((), ())), preferred_element_type=F32)


def _inproj_kernel(x_ref, g_ref, w_ref, seg_ref, gain_ref, cosa_ref, sina_ref, cosb_ref, sinb_ref,
                   qa_ref, qb_ref, ka_ref, kb_ref, va_ref, vb_ref):
    h = _rms(x_ref[...], g_ref[...]).astype(BF16)
    z = _dot(h, w_ref[...])
    lane = lax.broadcasted_iota(jnp.int32, (1, LANES), 1)
    seg = seg_ref[...]

    def norm_rope(c, cos, sin, first, half):
        zc = z[:, c * LANES:(c + 1) * LANES]
        sq = zc * zc
        hi = sq.astype(BF16)
        lo = (sq - hi.astype(F32)).astype(BF16)
        ms = (_dot(hi, seg) + _dot(lo, seg)) * (1.0 / HEAD_DIM)
        y = zc * lax.rsqrt(ms + EPS) * gain_ref[:, c * LANES:(c + 1) * LANES]
        rot = jnp.where(first, pltpu.roll(y, LANES - half, 1), pltpu.roll(y, half, 1))
        return y * cos + rot * sin

    first_a = (lane % HEAD_DIM) < (HEAD_DIM // 2)
    first_b = (lane % (HEAD_DIM // 2)) < (HEAD_DIM // 4)
    cosa, sina, cosb, sinb = cosa_ref[...], sina_ref[...], cosb_ref[...], sinb_ref[...]
    nq = Q_WIDTH // LANES
    for c in range(nq):
        qa_ref[:, c * LANES:(c + 1) * LANES] = norm_rope(c, cosa, sina, first_a, HEAD_DIM // 2).astype(BF16)
        qb_ref[:, c * LANES:(c + 1) * LANES] = norm_rope(nq + c, cosb, sinb, first_b, HEAD_DIM // 4).astype(BF16)
    ka_ref[...] = norm_rope(2 * nq, cosa, sina, first_a, HEAD_DIM // 2).astype(BF16)
    kb_ref[...] = norm_rope(2 * nq + 1, cosb, sinb, first_b, HEAD_DIM // 4).astype(BF16)
    va_ref[...] = z[:, NORM_COLS:NORM_COLS + KV_WIDTH].astype(BF16)
    vb_ref[...] = z[:, NORM_COLS + KV_WIDTH:].astype(BF16)


def _rope_tables(seq):
    lane = np.arange(LANES)
    t = jnp.arange(seq, dtype=jnp.int32)
    inv_a = ROPE_THETA ** (-jnp.arange(0, HEAD_DIM, 2, dtype=F32) / HEAD_DIM)
    ang_a = t.astype(F32)[:, None] * inv_a[None, :]
    idx_a = lane % (HEAD_DIM // 2)
    sign_a = np.where((lane % HEAD_DIM) < HEAD_DIM // 2, -1.0, 1.0).astype(np.float32)
    cos_a = jnp.cos(ang_a)[:, idx_a]
    sin_a = jnp.sin(ang_a)[:, idx_a] * sign_a[None, :]
    hd2 = HEAD_DIM // 2
    inv_b = ROPE_THETA ** (-jnp.arange(0, hd2, 2, dtype=F32) / hd2)
    row = (t // GRID_W).astype(F32)
    col = (t % GRID_W).astype(F32)
    ang_row = row[:, None] * inv_b[None, :]
    ang_col = col[:, None] * inv_b[None, :]
    idx_b = lane % (hd2 // 2)
    is_row = ((lane % HEAD_DIM) < hd2)[None, :]
    sign_b = np.where((lane % hd2) < hd2 // 2, -1.0, 1.0).astype(np.float32)
    cos_b = jnp.where(is_row, jnp.cos(ang_row)[:, idx_b], jnp.cos(ang_col)[:, idx_b])
    sin_b = jnp.where(is_row, jnp.sin(ang_row)[:, idx_b], jnp.sin(ang_col)[:, idx_b]) * sign_b[None, :]
    return cos_a, sin_a, cos_b, sin_b


def _inproj(x2d, seq, ln_mix, w_in_p, seg, gain, tables, tm):
    t_total = x2d.shape[0]
    nblk_seq = seq // tm
    tok = lambda w: pl.BlockSpec((tm, w), lambda i: (i, 0))
    const = lambda a: pl.BlockSpec(a.shape, lambda i: (0,) * a.ndim)
    tab = pl.BlockSpec((tm, LANES), lambda i: (i % nblk_seq, 0))
    outs = [jax.ShapeDtypeStruct((t_total, w), BF16) for w in
            (Q_WIDTH, Q_WIDTH, KV_WIDTH, KV_WIDTH, KV_WIDTH, KV_WIDTH)]
    return pl.pallas_call(
        _inproj_kernel,
        out_shape=outs,
        grid=(t_total // tm,),
        in_specs=[tok(D_MODEL), const(ln_mix), const(w_in_p), const(seg), const(gain), tab, tab, tab, tab],
        out_specs=[tok(Q_WIDTH), tok(Q_WIDTH), tok(KV_WIDTH), tok(KV_WIDTH), tok(KV_WIDTH), tok(KV_WIDTH)],
        compiler_params=_params("parallel"),
        name="inproj",
    )(x2d, ln_mix, w_in_p, seg, gain, *tables)


def _attn_a_kernel(q_ref, k_ref, v_ref, sink_ref, go_ref, o_ref, *, seq):
    bq = WINDOW
    span = bq + 2 * WINDOW
    qi = pl.program_id(1)
    start = qi * bq
    c = jnp.clip(start - WINDOW, 0, seq - span)
    c = pl.multiple_of(c, bq)
    q = q_ref[0]
    k = k_ref[0, pl.ds(c, span), :]
    v = v_ref[0, pl.ds(c, span), :]
    qpos = start + lax.broadcasted_iota(jnp.int32, (bq, 1), 0)
    kpos = c + lax.broadcasted_iota(jnp.int32, (1, span), 1)
    valid = jnp.abs(qpos - kpos) <= WINDOW
    outs = []
    for h in range(Q_HEADS):
        g = h // GROUP
        qh = q[:, h * HEAD_DIM:(h + 1) * HEAD_DIM]
        kg = k[:, g * HEAD_DIM:(g + 1) * HEAD_DIM]
        vg = v[:, g * HEAD_DIM:(g + 1) * HEAD_DIM]
        s = jnp.where(valid, _dot_nt(qh, kg), NEG)
        sink = sink_ref[:, h:h + 1]
        m = jnp.maximum(jnp.max(s, axis=-1, keepdims=True), sink)
        p = jnp.exp(s - m)
        den = jnp.sum(p, axis=-1, keepdims=True) + jnp.exp(sink - m)
        outs.append(_dot(p.astype(BF16), vg) / den)
    o = jnp.concatenate(outs, axis=-1)
    o_ref[0] = _rms(o, go_ref[...]).astype(BF16)


def _attn_a(q, k, v, sink, go):
    b, seq, _ = q.shape
    assert seq % WINDOW == 0 and seq >= 3 * WINDOW
    return pl.pallas_call(
        functools.partial(_attn_a_kernel, seq=seq),
        out_shape=jax.ShapeDtypeStruct((b, seq, Q_WIDTH), BF16),
        grid=(b, seq // WINDOW),
        in_specs=[pl.BlockSpec((1, WINDOW, Q_WIDTH), lambda i, j: (i, j, 0)),
                  pl.BlockSpec((1, seq, KV_WIDTH), lambda i, j: (i, 0, 0)),
                  pl.BlockSpec((1, seq, KV_WIDTH), lambda i, j: (i, 0, 0)),
                  pl.BlockSpec(sink.shape, lambda i, j: (0, 0)),
                  pl.BlockSpec(go.shape, lambda i, j: (0, 0))],
        out_specs=pl.BlockSpec((1, WINDOW, Q_WIDTH), lambda i, j: (i, j, 0)),
        compiler_params=_params("parallel", "arbitrary"),
        name="attn_a",
    )(q, k, v, sink, go)


def _attn_b_kernel(q_ref, k_ref, v_ref, go_ref, o_ref, *, seq, tq, tk):
    q = q_ref[0]
    outs = []
    for g in range(KV_HEADS):
        q4 = jnp.concatenate(
            [q[:, (g * GROUP + j) * HEAD_DIM:(g * GROUP + j + 1) * HEAD_DIM] for j in range(GROUP)], axis=0)

        def body(kb, carry, g=g, q4=q4):
            m, l, acc = carry
            off = pl.multiple_of(kb * tk, tk)
            k = k_ref[0, pl.ds(off, tk), g * HEAD_DIM:(g + 1) * HEAD_DIM]
            v = v_ref[0, pl.ds(off, tk), g * HEAD_DIM:(g + 1) * HEAD_DIM]
            s = _dot_nt(q4, k)
            m_new = jnp.maximum(m, jnp.max(s, axis=-1, keepdims=True))
            alpha = jnp.exp(m - m_new)
            p = jnp.exp(s - m_new)
            l = alpha * l + jnp.sum(p, axis=-1, keepdims=True)
            acc = alpha * acc + _dot(p.astype(BF16), v)
            return m_new, l, acc

        init = (jnp.full((GROUP * tq, 1), NEG, F32), jnp.zeros((GROUP * tq, 1), F32),
                jnp.zeros((GROUP * tq, HEAD_DIM), F32))
        _, l, acc = lax.fori_loop(0, seq // tk, body, init)
        og = acc / l
        outs += [og[j * tq:(j + 1) * tq] for j in range(GROUP)]
    o = jnp.concatenate(outs, axis=-1)
    o_ref[0] = _rms(o, go_ref[...]).astype(BF16)


def _attn_b(q, k, v, go, tq, tk):
    b, seq, _ = q.shape
    return pl.pallas_call(
        functools.partial(_attn_b_kernel, seq=seq, tq=tq, tk=tk),
        out_shape=jax.ShapeDtypeStruct((b, seq, Q_WIDTH), BF16),
        grid=(b, seq // tq),
        in_specs=[pl.BlockSpec((1, tq, Q_WIDTH), lambda i, j: (i, j, 0)),
                  pl.BlockSpec((1, seq, KV_WIDTH), lambda i, j: (i, 0, 0)),
                  pl.BlockSpec((1, seq, KV_WIDTH), lambda i, j: (i, 0, 0)),
                  pl.BlockSpec(go.shape, lambda i, j: (0, 0))],
        out_specs=pl.BlockSpec((1, tq, Q_WIDTH), lambda i, j: (i, j, 0)),
        compiler_params=_params("parallel", "arbitrary"),
        name="attn_b",
    )(q, k, v, go)


def _memkv_kernel(mem_ref, g_ref, w_ref, ckn_ref, k_ref, v_ref):
    h = _rms(mem_ref[0], g_ref[...]).astype(BF16)
    kv = _dot(h, w_ref[...])
    ckn = ckn_ref[...]
    for hd in range(X_HEADS):
        sl = slice(hd * X_HEAD_DIM, (hd + 1) * X_HEAD_DIM)
        k_ref[0, :, sl] = _rms(kv[:, sl], ckn).astype(BF16)
    v_ref[0] = kv[:, X_WIDTH:].astype(BF16)


def _memkv(mem, ln_mem, w_ckv, ckn):
    b = mem.shape[0]
    const = lambda a: pl.BlockSpec(a.shape, lambda i: (0,) * a.ndim)
    return pl.pallas_call(
        _memkv_kernel,
        out_shape=[jax.ShapeDtypeStruct((b, N_MEM, X_WIDTH), BF16)] * 2,
        grid=(b,),
        in_specs=[pl.BlockSpec((1, N_MEM, D_MODEL), lambda i: (i, 0, 0)), const(ln_mem), const(w_ckv), const(ckn)],
        out_specs=[pl.BlockSpec((1, N_MEM, X_WIDTH), lambda i: (i, 0, 0))] * 2,
        compiler_params=_params("parallel"),
        name="memkv",
    )(mem, ln_mem, w_ckv, ckn)


def _mid_kernel(x_ref, oa_ref, ob_ref, woa_ref, wob_ref, lnx_ref, wcq_ref, cqn_ref, kc_ref, vc_ref, wco_ref,
                o_ref):
    x1 = x_ref[...] + _dot(oa_ref[...], woa_ref[...]) + _dot(ob_ref[...], wob_ref[...])
    h = _rms(x1, lnx_ref[...]).astype(BF16)
    qc = _dot(h, wcq_ref[...])
    cqn = cqn_ref[...]
    outs = []
    for hd in range(X_HEADS):
        sl = slice(hd * X_HEAD_DIM, (hd + 1) * X_HEAD_DIM)
        qn = _rms(qc[:, sl], cqn).astype(BF16)
        s = _dot_nt(qn, kc_ref[0, :, sl])
        m = jnp.max(s, axis=-1, keepdims=True)
        p = jnp.exp(s - m)
        den = jnp.sum(p, axis=-1, keepdims=True)
        outs.append((_dot(p.astype(BF16), vc_ref[0, :, sl]) / den).astype(BF16))
    o = jnp.concatenate(outs, axis=-1)
    o_ref[...] = x1 + _dot(o, wco_ref[...])


def _mid(x2d, seq, oa, ob, w_out_a, w_out_b, ln_x, w_cq, cqn_s, kc, vc, w_co, tm):
    t_total = x2d.shape[0]
    nblk_seq = seq // tm
    tok = lambda w: pl.BlockSpec((tm, w), lambda i: (i, 0))
    const = lambda a: pl.BlockSpec(a.shape, lambda i: (0,) * a.ndim)
    memspec = pl.BlockSpec((1, N_MEM, X_WIDTH), lambda i: (i // nblk_seq, 0, 0))
    return pl.pallas_call(
        _mid_kernel,
        out_shape=jax.ShapeDtypeStruct((t_total, D_MODEL), F32),
        grid=(t_total // tm,),
        in_specs=[tok(D_MODEL), tok(Q_WIDTH), tok(Q_WIDTH), const(w_out_a), const(w_out_b), const(ln_x),
                  const(w_cq), const(cqn_s), memspec, memspec, const(w_co)],
        out_specs=tok(D_MODEL),
        compiler_params=_params("parallel"),
        name="mid",
    )(x2d, oa, ob, w_out_a, w_out_b, ln_x, w_cq, cqn_s, kc, vc, w_co)


def _oddeven_merge(lo, hi, r):
    step = r * 2
    if step < hi - lo:
        yield from _oddeven_merge(lo, hi, step)
        yield from _oddeven_merge(lo + r, hi, step)
        yield from [(i, i + r) for i in range(lo + r, hi - r, step)]
    else:
        yield (lo, lo + r)


def _oddeven_sort_pairs(lo, hi):
    if hi - lo >= 1:
        mid = lo + (hi - lo) // 2
        yield from _oddeven_sort_pairs(lo, mid)
        yield from _oddeven_sort_pairs(mid + 1, hi)
        yield from _oddeven_merge(lo, hi, 1)


_SORT16 = tuple(_oddeven_sort_pairs(0, PEER_TOPK - 1))
_BITONIC16 = tuple((i, i + d) for d in (8, 4, 2, 1) for i in range(PEER_TOPK) if not i & d)


def _apply_network(xs, pairs):
    xs = list(xs)
    for i, j in pairs:
        xs[i], xs[j] = jnp.maximum(xs[i], xs[j]), jnp.minimum(xs[i], xs[j])
    return xs


def _top_half(xs, ys):
    n = len(xs)
    return [jnp.maximum(xs[a], ys[n - 1 - a]) for a in range(n)]


def _top16_sorted(x):
    rows = [x[a * SUBLANES:(a + 1) * SUBLANES, :] for a in range(PEER_NKEYS // SUBLANES)]
    rows = _apply_network(rows, _SORT16)
    for shift in (4, 2, 1):
        other = [pltpu.roll(r, shift, 0) for r in rows]
        rows = _apply_network(_top_half(rows, other), _BITONIC16)
    return rows


def _top16_products(e1, e2):
    k = PEER_TOPK
    row = lambda r: [e1[r] * e2[q] for q in range(k // (r + 1))]
    a = row(0)
    col = [e1[r] * e2[0] for r in range(8, 16)]
    b = _apply_network(row(1) + col[::-1], _BITONIC16)
    c = _apply_network(row(2) + row(3) + row(4) + row(5) + row(6), _SORT16)
    d = row(7)
    t = _apply_network(_top_half(a, b), _BITONIC16)
    t = _apply_network(_top_half(t, c), _BITONIC16)
    t[k - 1] = jnp.maximum(t[k - 1], d[0])
    t[k - 2] = jnp.maximum(t[k - 2], d[1])
    return t


def _tree(op, xs):
    xs = list(xs)
    while len(xs) > 1:
        xs = [op(xs[i], xs[i + 1]) for i in range(0, len(xs) - 1, 2)] + ([xs[-1]] if len(xs) % 2 else [])
    return xs[0]


def _pprep_kernel(x_ref, g_ref, wpq_ref, pk1_ref, pk2_ref, ht_ref, e1_ref, e2_ref, thr_ref, p1_ref, p2_ref, rz_ref,
                  *, tm):
    hf = _rms(x_ref[...], g_ref[...])
    h = hf.astype(BF16)
    ht_ref[...] = hf.T.astype(BF16)
    half = PEER_QDIM // 2
    ngrp = tm // LANES

    def head(hd, carry):
        q = _dot(h, wpq_ref[hd]).astype(BF16)
        s1 = _dot_nt(pk1_ref[hd], q[:, :half])
        s2 = _dot_nt(pk2_ref[hd], q[:, half:])
        e1 = jnp.exp(s1 - jnp.max(s1, axis=0, keepdims=True))
        e2 = jnp.exp(s2 - jnp.max(s2, axis=0, keepdims=True))
        r0 = pl.multiple_of(hd * PEER_NKEYS, PEER_NKEYS)
        e1_ref[pl.ds(r0, PEER_NKEYS), :] = e1
        e2_ref[pl.ds(r0, PEER_NKEYS), :] = e2
        for lg in range(ngrp):
            ls = slice(lg * LANES, (lg + 1) * LANES)
            t1 = _top16_sorted(e1[:, ls])
            t2 = _top16_sorted(e2[:, ls])
            for r in range(PEER_TOPK):
                p1_ref[hd, r, :, ls] = t1[r]
                p2_ref[hd, r, :, ls] = t2[r]
        return carry

    lax.fori_loop(0, PEER_HEADS, head, 0)

    sub = lax.broadcasted_iota(jnp.int32, (SUBLANES, LANES), 0)

    def packed(ref, r, ls):
        out = ref[0, r, :, ls]
        for hd in range(1, PEER_HEADS):
            out = jnp.where(sub == hd, ref[hd, r, :, ls], out)
        return out

    for lg in range(ngrp):
        ls = slice(lg * LANES, (lg + 1) * LANES)
        t1 = [packed(p1_ref, r, ls) for r in range(PEER_TOPK)]
        t2 = [packed(p2_ref, r, ls) for r in range(PEER_TOPK)]
        rz = 1.0 / _tree(jnp.add, _top16_products(t1, t2))
        t1n = [a * rz for a in t1]
        thr_ref[:, ls] = _tree(jnp.minimum, _top16_products(t1n, t2))
        rz_ref[:, ls] = rz

    for hd in range(PEER_HEADS):
        rs = slice(hd * PEER_NKEYS, (hd + 1) * PEER_NKEYS)
        e1_ref[rs, :] = e1_ref[rs, :] * rz_ref[hd:hd + 1, :]


def _pprep(x2, ln_ff, wpq_h, pk1, pk2, tm):
    t_total = x2.shape[0]
    const = lambda a: pl.BlockSpec(a.shape, lambda i: (0,) * a.ndim)
    hk = PEER_HEADS * PEER_NKEYS
    return pl.pallas_call(
        functools.partial(_pprep_kernel, tm=tm),
        out_shape=[jax.ShapeDtypeStruct((D_MODEL, t_total), BF16),
                   jax.ShapeDtypeStruct((hk, t_total), F32),
                   jax.ShapeDtypeStruct((hk, t_total), F32),
                   jax.ShapeDtypeStruct((PEER_HEADS, t_total), F32)],
        grid=(t_total // tm,),
        in_specs=[pl.BlockSpec((tm, D_MODEL), lambda i: (i, 0)), const(ln_ff), const(wpq_h), const(pk1), const(pk2)],
        out_specs=[pl.BlockSpec((D_MODEL, tm), lambda i: (0, i)),
                   pl.BlockSpec((hk, tm), lambda i: (0, i)),
                   pl.BlockSpec((hk, tm), lambda i: (0, i)),
                   pl.BlockSpec((PEER_HEADS, tm), lambda i: (0, i))],
        scratch_shapes=[pltpu.VMEM((PEER_HEADS, PEER_TOPK, SUBLANES, tm), F32),
                        pltpu.VMEM((PEER_HEADS, PEER_TOPK, SUBLANES, tm), F32),
                        pltpu.VMEM((PEER_HEADS, tm), F32)],
        compiler_params=_params("parallel"),
        name="pprep",
    )(x2, ln_ff, wpq_h, pk1, pk2)


def _pdense_kernel(x_ref, ht_ref, e1_ref, e2_ref, thr_ref, u_ref, vt_ref, o_ref, acc_ref, a_ref, wg_ref, *, rows):
    e = pl.program_id(1)
    tm = acc_ref.shape[1]
    slab = 2 * SUBLANES

    @pl.when(e == 0)
    def _():
        acc_ref[...] = jnp.zeros_like(acc_ref)

    a_ref[...] = _dot(u_ref[...], ht_ref[...])
    for r in range(rows):
        for lg in range(tm // LANES):
            ls = slice(lg * LANES, (lg + 1) * LANES)
            e1b = [jnp.broadcast_to(e1_ref[hd, r:r + 1, ls], (slab, LANES)) for hd in range(PEER_HEADS)]
            thb = [jnp.broadcast_to(thr_ref[hd:hd + 1, ls], (slab, LANES)) for hd in range(PEER_HEADS)]
            for sg in range(PEER_NKEYS // slab):
                w = None
                for hd in range(PEER_HEADS):
                    p = e1b[hd] * e2_ref[hd * PEER_NKEYS + sg * slab:hd * PEER_NKEYS + (sg + 1) * slab, ls]
                    sel = jnp.where(p >= thb[hd], p, 0.0)
                    w = sel if w is None else w + sel
                ss = slice(r * PEER_NKEYS + sg * slab, r * PEER_NKEYS + (sg + 1) * slab)
                a = a_ref[ss, ls]
                act = 0.5 * a * (1.0 + lax.erf(a * (2.0 ** -0.5)))
                wg_ref[ss, ls] = (w * act).astype(BF16)
    acc_ref[...] += _dot(vt_ref[...], wg_ref[...])

    @pl.when(e == pl.num_programs(1) - 1)
    def _():
        o_ref[...] = x_ref[...] + acc_ref[...].T


def _pdense(x2, ht, e1n, e2, thr, u, vt, tm, rows):
    t_total = x2.shape[0]
    en = rows * PEER_NKEYS
    hk = PEER_HEADS * PEER_NKEYS
    assert rows % SUBLANES == 0
    e1n = e1n.reshape(PEER_HEADS, PEER_NKEYS, t_total)
    return pl.pallas_call(
        functools.partial(_pdense_kernel, rows=rows),
        out_shape=jax.ShapeDtypeStruct((t_total, D_MODEL), F32),
        grid=(t_total // tm, PEER_EXPERTS // en),
        in_specs=[pl.BlockSpec((tm, D_MODEL), lambda i, j: (i, 0)),
                  pl.BlockSpec((D_MODEL, tm), lambda i, j: (0, i)),
                  pl.BlockSpec((PEER_HEADS, rows, tm), lambda i, j: (0, j, i)),
                  pl.BlockSpec((hk, tm), lambda i, j: (0, i)),
                  pl.BlockSpec((PEER_HEADS, tm), lambda i, j: (0, i)),
                  pl.BlockSpec((en, D_MODEL), lambda i, j: (j, 0)),
                  pl.BlockSpec((D_MODEL, en), lambda i, j: (0, j))],
        out_specs=pl.BlockSpec((tm, D_MODEL), lambda i, j: (i, 0)),
        scratch_shapes=[pltpu.VMEM((D_MODEL, tm), F32), pltpu.VMEM((en, tm), F32), pltpu.VMEM((en, tm), BF16)],
        compiler_params=_params("parallel", "arbitrary"),
        name="pdense",
    )(x2, ht, e1n, e2, thr, u, vt)


def _prepare_weights(ln_mix, w_in, qn_a, kn_a, sink_a, qn_b, kn_b, go_a, go_b, w_out, ln_x, ln_mem, w_cq, w_ckv,
                     cqn, ckn, w_co, ln_ff, w_pq, pk1, pk2, peer_u, peer_v):
    row = lambda a: a.reshape(1, -1).astype(F32)
    qa0, ka0, va0 = 0, Q_WIDTH, Q_WIDTH + KV_WIDTH
    qb0 = Q_WIDTH + 2 * KV_WIDTH
    kb0, vb0 = qb0 + Q_WIDTH, qb0 + Q_WIDTH + KV_WIDTH
    cols = lambda s, w: w_in[:, s:s + w]
    w_in_p = jnp.concatenate([cols(qa0, Q_WIDTH), cols(qb0, Q_WIDTH), cols(ka0, KV_WIDTH), cols(kb0, KV_WIDTH),
                              cols(va0, KV_WIDTH), cols(vb0, KV_WIDTH)], axis=1).astype(BF16)
    scale = HEAD_DIM ** -0.5
    gain = jnp.concatenate([jnp.tile(qn_a, Q_HEADS) * scale, jnp.tile(qn_b, Q_HEADS) * scale,
                            jnp.tile(kn_a, KV_HEADS), jnp.tile(kn_b, KV_HEADS)]).reshape(1, NORM_COLS).astype(F32)
    lane = np.arange(LANES)
    seg = jnp.asarray((lane[:, None] // HEAD_DIM) == (lane[None, :] // HEAD_DIM), dtype=BF16)
    return dict(
        ln_mix=row(ln_mix), w_in_p=w_in_p, gain=gain, seg=seg, sink=row(sink_a), go_a=row(go_a), go_b=row(go_b),
        w_out_a=w_out[:Q_WIDTH].astype(BF16), w_out_b=w_out[Q_WIDTH:].astype(BF16),
        ln_x=row(ln_x), ln_mem=row(ln_mem), w_cq=w_cq.astype(BF16), w_ckv=w_ckv.astype(BF16),
        cqn_s=row(cqn) * (X_HEAD_DIM ** -0.5), ckn=row(ckn), w_co=w_co.astype(BF16), ln_ff=row(ln_ff),
        wpq_h=w_pq.reshape(D_MODEL, PEER_HEADS, PEER_QDIM).transpose(1, 0, 2).astype(BF16),
        pk1=pk1.astype(BF16), pk2=pk2.astype(BF16), u=peer_u.astype(BF16), vt=peer_v.T.astype(BF16))


def _layer(x, mem, p, *, tm_in=512, tq_b=256, tk_b=512, tm_mid=512, tm_prep=256, tm_dense=512, rows=8):
    b, seq, _ = x.shape
    x2d = x.reshape(b * seq, D_MODEL)
    tables = _rope_tables(seq)
    qa, qb, ka, kb, va, vb = _inproj(x2d, seq, p["ln_mix"], p["w_in_p"], p["seg"], p["gain"], tables, tm_in)
    r3 = lambda a: a.reshape(b, seq, a.shape[-1])
    oa = _attn_a(r3(qa), r3(ka), r3(va), p["sink"], p["go_a"])
    ob = _attn_b(r3(qb), r3(kb), r3(vb), p["go_b"], tq_b, tk_b)
    kc, vc = _memkv(mem, p["ln_mem"], p["w_ckv"], p["ckn"])
    x2 = _mid(x2d, seq, oa.reshape(b * seq, Q_WIDTH), ob.reshape(b * seq, Q_WIDTH), p["w_out_a"], p["w_out_b"],
              p["ln_x"], p["w_cq"], p["cqn_s"], kc, vc, p["w_co"], tm_mid)
    h3, e1n, e2, thr = _pprep(x2, p["ln_ff"], p["wpq_h"], p["pk1"], p["pk2"], tm_prep)
    y = _pdense(x2, h3, e1n, e2, thr, p["u"], p["vt"], tm_dense, rows)
    return y.reshape(b, seq, D_MODEL)


def kernel(x_prompt, x_sample, mem_prompt, mem_sample, ln_mix, w_in, qn_a, kn_a, sink_a, qn_b, kn_b, go_a, go_b,
           w_out, ln_x, ln_mem, w_cq, w_ckv, cqn, ckn, w_co, ln_ff, w_pq, pk1, pk2, peer_u, peer_v):
    params = (ln_mix, w_in, qn_a, kn_a, sink_a, qn_b, kn_b, go_a, go_b, w_out, ln_x, ln_mem, w_cq, w_ckv,
              cqn, ckn, w_co, ln_ff, w_pq, pk1, pk2, peer_u, peer_v)
    y_prompt, y_sample = x_prompt, x_sample
    for l in range(ln_mix.shape[0]):
        p = _prepare_weights(*[a[l] for a in params])
        y_prompt = _layer(y_prompt, mem_prompt, p)
        y_sample = _layer(y_sample, mem_sample, p)
    return (y_prompt, y_sample)
```

```python
import functools

import numpy as np
import jax
import jax.numpy as jnp
from jax import lax
from jax.experimental import pallas as pl
from jax.experimental.pallas import tpu as pltpu

F32 = jnp.float32
BF16 = jnp.bfloat16

D_MODEL = 1024
HEAD_DIM = 64
Q_HEADS = 8
KV_HEADS = 2
GROUP = Q_HEADS // KV_HEADS
Q_WIDTH = Q_HEADS * HEAD_DIM
KV_WIDTH = KV_HEADS * HEAD_DIM
IN_COLS = 2 * Q_WIDTH + 4 * KV_WIDTH
NORM_COLS = 2 * Q_WIDTH + 2 * KV_WIDTH
WINDOW = 128
ROPE_THETA = 10000.0
GRID_W = 64
N_MEM = 256
X_HEADS = 4
X_HEAD_DIM = 128
X_WIDTH = X_HEADS * X_HEAD_DIM
PEER_HEADS = 8
PEER_NKEYS = 128
PEER_EXPERTS = PEER_NKEYS * PEER_NKEYS
PEER_QDIM = 256
PEER_TOPK = 16
EPS = 1e-6
NEG = -1e30
LOG2_E = 1.4426950408889634

LANES = 128
SUBLANES = 8
VMEM_LIMIT = 56 * 1024 * 1024


def _params(*sem):
    return pltpu.CompilerParams(dimension_semantics=sem, vmem_limit_bytes=VMEM_LIMIT)


def _rms(x, g):
    ms = jnp.mean(x * x, axis=-1, keepdims=True)
    return x * lax.rsqrt(ms + EPS) * g


def _dot(a, b):
    return jnp.dot(a, b, preferred_element_type=F32)


def _dot_nt(a, b):
    return lax.dot_general(a, b, (((1,), (1,)), ((), ())), preferred_element_type=F32)


def _inproj_kernel(x_ref, g_ref, w_ref, seg_ref, gain_ref, cosa_ref, sina_ref, cosb_ref, sinb_ref,
                   qa_ref, qb_ref, ka_ref, kb_ref, va_ref, vb_ref):
    h = _rms(x_ref[...], g_ref[...]).astype(BF16)
    z = _dot(h, w_ref[...])
    lane = lax.broadcasted_iota(jnp.int32, (1, LANES), 1)
    seg = seg_ref[...]

    def norm_rope(c, cos, sin, first, half):
        zc = z[:, c * LANES:(c + 1) * LANES]
        sq = zc * zc
        hi = sq.astype(BF16)
        lo = (sq - hi.astype(F32)).astype(BF16)
        ms = (_dot(hi, seg) + _dot(lo, seg)) * (1.0 / HEAD_DIM)
        y = zc * lax.rsqrt(ms + EPS) * gain_ref[:, c * LANES:(c + 1) * LANES]
        rot = jnp.where(first, pltpu.roll(y, LANES - half, 1), pltpu.roll(y, half, 1))
        return y * cos + rot * sin

    first_a = (lane % HEAD_DIM) < (HEAD_DIM // 2)
    first_b = (lane % (HEAD_DIM // 2)) < (HEAD_DIM // 4)
    cosa, sina, cosb, sinb = cosa_ref[...], sina_ref[...], cosb_ref[...], sinb_ref[...]
    nq = Q_WIDTH // LANES
    for c in range(nq):
        qa_ref[c * LANES:(c + 1) * LANES, :] = norm_rope(c, cosa, sina, first_a, HEAD_DIM // 2).T.astype(BF16)
        qb_ref[c * LANES:(c + 1) * LANES, :] = norm_rope(nq + c, cosb, sinb, first_b, HEAD_DIM // 4).T.astype(BF16)
    ka_ref[...] = norm_rope(2 * nq, cosa, sina, first_a, HEAD_DIM // 2).astype(BF16)
    kb_ref[...] = norm_rope(2 * nq + 1, cosb, sinb, first_b, HEAD_DIM // 4).astype(BF16)
    va_ref[...] = z[:, NORM_COLS:NORM_COLS + KV_WIDTH].astype(BF16)
    vb_ref[...] = z[:, NORM_COLS + KV_WIDTH:].astype(BF16)


def _rope_tables(seq):
    lane = np.arange(LANES)
    t = jnp.arange(seq, dtype=jnp.int32)
    inv_a = ROPE_THETA ** (-jnp.arange(0, HEAD_DIM, 2, dtype=F32) / HEAD_DIM)
    ang_a = t.astype(F32)[:, None] * inv_a[None, :]
    idx_a = lane % (HEAD_DIM // 2)
    sign_a = np.where((lane % HEAD_DIM) < HEAD_DIM // 2, -1.0, 1.0).astype(np.float32)
    cos_a = jnp.cos(ang_a)[:, idx_a]
    sin_a = jnp.sin(ang_a)[:, idx_a] * sign_a[None, :]
    hd2 = HEAD_DIM // 2
    inv_b = ROPE_THETA ** (-jnp.arange(0, hd2, 2, dtype=F32) / hd2)
    row = (t // GRID_W).astype(F32)
    col = (t % GRID_W).astype(F32)
    ang_row = row[:, None] * inv_b[None, :]
    ang_col = col[:, None] * inv_b[None, :]
    idx_b = lane % (hd2 // 2)
    is_row = ((lane % HEAD_DIM) < hd2)[None, :]
    sign_b = np.where((lane % hd2) < hd2 // 2, -1.0, 1.0).astype(np.float32)
    cos_b = jnp.where(is_row, jnp.cos(ang_row)[:, idx_b], jnp.cos(ang_col)[:, idx_b])
    sin_b = jnp.where(is_row, jnp.sin(ang_row)[:, idx_b], jnp.sin(ang_col)[:, idx_b]) * sign_b[None, :]
    return cos_a, sin_a, cos_b, sin_b


def _inproj(x2d, seq, ln_mix, w_in_p, seg, gain, tables, tm):
    t_total = x2d.shape[0]
    nblk_seq = seq // tm
    tok = lambda w: pl.BlockSpec((tm, w), lambda i: (i, 0))
    const = lambda a: pl.BlockSpec(a.shape, lambda i: (0,) * a.ndim)
    tab = pl.BlockSpec((tm, LANES), lambda i: (i % nblk_seq, 0))
    qt = pl.BlockSpec((Q_WIDTH, tm), lambda i: (0, i))
    outs = ([jax.ShapeDtypeStruct((Q_WIDTH, t_total), BF16)] * 2
            + [jax.ShapeDtypeStruct((t_total, KV_WIDTH), BF16)] * 4)
    return pl.pallas_call(
        _inproj_kernel,
        out_shape=outs,
        grid=(t_total // tm,),
        in_specs=[tok(D_MODEL), const(ln_mix), const(w_in_p), const(seg), const(gain), tab, tab, tab, tab],
        out_specs=[qt, qt, tok(KV_WIDTH), tok(KV_WIDTH), tok(KV_WIDTH), tok(KV_WIDTH)],
        compiler_params=_params("parallel"),
        name="inproj",
    )(x2d, ln_mix, w_in_p, seg, gain, *tables)


def _group_queries(qt_ref, g):
    heads = [qt_ref[(g * GROUP + j) * HEAD_DIM:(g * GROUP + j + 1) * HEAD_DIM, :] for j in range(GROUP)]
    qg = jnp.concatenate(heads, axis=1)
    parts = [qg if kv == g else jnp.zeros_like(qg) for kv in range(KV_HEADS)]
    return jnp.concatenate(parts, axis=0)


def _softmax_block(s_ref, p_ref, m_prev):
    rows, w = s_ref.shape
    slab = 2 * SUBLANES
    maxes, sums = [], []
    for lg in range(w // LANES):
        ls = slice(lg * LANES, (lg + 1) * LANES)
        mx = s_ref[0:SUBLANES, ls]
        for sl in range(1, rows // SUBLANES):
            mx = jnp.maximum(mx, s_ref[sl * SUBLANES:(sl + 1) * SUBLANES, ls])
        maxes.append(jnp.max(mx, axis=0, keepdims=True))
    m_new = jnp.maximum(m_prev, jnp.concatenate(maxes, axis=1))
    for lg in range(w // LANES):
        ls = slice(lg * LANES, (lg + 1) * LANES)
        mrow = jnp.broadcast_to(m_new[:, ls], (slab, LANES))
        tot = None
        for sl in range(rows // slab):
            p = jnp.exp2(s_ref[sl * slab:(sl + 1) * slab, ls] - mrow)
            tot = p if tot is None else tot + p
            p_ref[sl * slab:(sl + 1) * slab, ls] = p.astype(BF16)
        sums.append(jnp.sum(tot, axis=0, keepdims=True))
    return m_new, jnp.concatenate(sums, axis=1)


def _dot_tn(a, b):
    return lax.dot_general(a, b, (((0,), (0,)), ((), ())), preferred_element_type=F32)


def _finish_heads(ot_ref, g, og, tq):
    for j in range(GROUP):
        h = g * GROUP + j
        ot_ref[h * HEAD_DIM:(h + 1) * HEAD_DIM, :] = og[:, j * tq:(j + 1) * tq]


def _attn_a_kernel(qt_ref, k_ref, v_ref, sink_ref, go_ref, o_ref, s_ref, p_ref, ot_ref, *, seq):
    bq = WINDOW
    span = bq + 2 * WINDOW
    w = GROUP * bq
    start = pl.program_id(1) * bq
    c = pl.multiple_of(jnp.clip(start - WINDOW, 0, seq - span), bq)
    k = k_ref[0, pl.ds(c, span), :]
    v = v_ref[0, pl.ds(c, span), :]
    kpos = c + lax.broadcasted_iota(jnp.int32, (span, 1), 0)
    qpos = start + lax.broadcasted_iota(jnp.int32, (1, w), 1) % bq
    valid = jnp.abs(kpos - qpos) <= WINDOW
    for g in range(KV_HEADS):
        s_ref[...] = jnp.where(valid, _dot(k, _group_queries(qt_ref, g)), NEG)
        sink = sink_ref[g:g + 1, :]
        m, tot = _softmax_block(s_ref, p_ref, sink)
        den = tot + jnp.exp2(sink - m)
        pv = _dot_tn(v, p_ref[...])
        _finish_heads(ot_ref, g, pv[g * HEAD_DIM:(g + 1) * HEAD_DIM, :] / den, bq)
    o_ref[0] = _rms(ot_ref[...].T, go_ref[...]).astype(BF16)


def _attn_a(qt, k, v, sink_w, go):
    b, seq, _ = k.shape
    assert seq % WINDOW == 0 and seq >= 3 * WINDOW
    nq = seq // WINDOW
    span = 3 * WINDOW
    w = GROUP * WINDOW
    return pl.pallas_call(
        functools.partial(_attn_a_kernel, seq=seq),
        out_shape=jax.ShapeDtypeStruct((b, seq, Q_WIDTH), BF16),
        grid=(b, nq),
        in_specs=[pl.BlockSpec((Q_WIDTH, WINDOW), lambda i, j: (0, i * nq + j)),
                  pl.BlockSpec((1, seq, KV_WIDTH), lambda i, j: (i, 0, 0)),
                  pl.BlockSpec((1, seq, KV_WIDTH), lambda i, j: (i, 0, 0)),
                  pl.BlockSpec(sink_w.shape, lambda i, j: (0, 0)),
                  pl.BlockSpec(go.shape, lambda i, j: (0, 0))],
        out_specs=pl.BlockSpec((1, WINDOW, Q_WIDTH), lambda i, j: (i, j, 0)),
        scratch_shapes=[pltpu.VMEM((span, w), F32), pltpu.VMEM((span, w), BF16), pltpu.VMEM((Q_WIDTH, WINDOW), F32)],
        compiler_params=_params("parallel", "arbitrary"),
        name="attn_a",
    )(qt, k, v, sink_w, go)


def _attn_b_kernel(qt_ref, k_ref, v_ref, go_ref, o_ref, s_ref, p_ref, acc_ref, ot_ref, *, seq, tq, tk):
    w = GROUP * tq
    for g in range(KV_HEADS):
        qz = _group_queries(qt_ref, g)
        acc_ref[...] = jnp.zeros_like(acc_ref)

        def body(kb, carry, qz=qz):
            m, l = carry
            off = pl.multiple_of(kb * tk, tk)
            s_ref[...] = _dot(k_ref[0, pl.ds(off, tk), :], qz)
            m_new, tot = _softmax_block(s_ref, p_ref, m)
            alpha = jnp.exp2(m - m_new)
            pv = _dot_tn(v_ref[0, pl.ds(off, tk), :], p_ref[...])
            acc_ref[...] = acc_ref[...] * alpha + pv
            return m_new, alpha * l + tot

        init = (jnp.full((1, w), NEG, F32), jnp.zeros((1, w), F32))
        _, l = lax.fori_loop(0, seq // tk, body, init)
        _finish_heads(ot_ref, g, acc_ref[g * HEAD_DIM:(g + 1) * HEAD_DIM, :] / l, tq)
    o_ref[0] = _rms(ot_ref[...].T, go_ref[...]).astype(BF16)


def _attn_b(qt, k, v, go, tq, tk):
    b, seq, _ = k.shape
    nq = seq // tq
    w = GROUP * tq
    return pl.pallas_call(
        functools.partial(_attn_b_kernel, seq=seq, tq=tq, tk=tk),
        out_shape=jax.ShapeDtypeStruct((b, seq, Q_WIDTH), BF16),
        grid=(b, nq),
        in_specs=[pl.BlockSpec((Q_WIDTH, tq), lambda i, j: (0, i * nq + j)),
                  pl.BlockSpec((1, seq, KV_WIDTH), lambda i, j: (i, 0, 0)),
                  pl.BlockSpec((1, seq, KV_WIDTH), lambda i, j: (i, 0, 0)),
                  pl.BlockSpec(go.shape, lambda i, j: (0, 0))],
        out_specs=pl.BlockSpec((1, tq, Q_WIDTH), lambda i, j: (i, j, 0)),
        scratch_shapes=[pltpu.VMEM((tk, w), F32), pltpu.VMEM((tk, w), BF16), pltpu.VMEM((KV_WIDTH, w), F32),
                        pltpu.VMEM((Q_WIDTH, tq), F32)],
        compiler_params=_params("parallel", "arbitrary"),
        name="attn_b",
    )(qt, k, v, go)


def _memkv_kernel(mem_ref, g_ref, w_ref, ckn_ref, k_ref, v_ref):
    h = _rms(mem_ref[0], g_ref[...]).astype(BF16)
    kv = _dot(h, w_ref[...])
    ckn = ckn_ref[...]
    for hd in range(X_HEADS):
        sl = slice(hd * X_HEAD_DIM, (hd + 1) * X_HEAD_DIM)
        k_ref[0, :, sl] = _rms(kv[:, sl], ckn).astype(BF16)
    v_ref[0] = kv[:, X_WIDTH:].astype(BF16)


def _memkv(mem, ln_mem, w_ckv, ckn):
    b = mem.shape[0]
    const = lambda a: pl.BlockSpec(a.shape, lambda i: (0,) * a.ndim)
    return pl.pallas_call(
        _memkv_kernel,
        out_shape=[jax.ShapeDtypeStruct((b, N_MEM, X_WIDTH), BF16)] * 2,
        grid=(b,),
        in_specs=[pl.BlockSpec((1, N_MEM, D_MODEL), lambda i: (i, 0, 0)), const(ln_mem), const(w_ckv), const(ckn)],
        out_specs=[pl.BlockSpec((1, N_MEM, X_WIDTH), lambda i: (i, 0, 0))] * 2,
        compiler_params=_params("parallel"),
        name="memkv",
    )(mem, ln_mem, w_ckv, ckn)


def _mid_kernel(x_ref, oa_ref, ob_ref, woa_ref, wob_ref, lnx_ref, wcq_ref, cqn_ref, kc_ref, vc_ref, wco_ref,
                o_ref):
    x1 = x_ref[...] + _dot(oa_ref[...], woa_ref[...]) + _dot(ob_ref[...], wob_ref[...])
    h = _rms(x1, lnx_ref[...]).astype(BF16)
    qc = _dot(h, wcq_ref[...])
    cqn = cqn_ref[...]
    outs = []
    for hd in range(X_HEADS):
        sl = slice(hd * X_HEAD_DIM, (hd + 1) * X_HEAD_DIM)
        qn = _rms(qc[:, sl], cqn).astype(BF16)
        s = _dot_nt(qn, kc_ref[0, :, sl])
        m = jnp.max(s, axis=-1, keepdims=True)
        p = jnp.exp(s - m)
        den = jnp.sum(p, axis=-1, keepdims=True)
        outs.append((_dot(p.astype(BF16), vc_ref[0, :, sl]) / den).astype(BF16))
    o = jnp.concatenate(outs, axis=-1)
    o_ref[...] = x1 + _dot(o, wco_ref[...])


def _mid(x2d, seq, oa, ob, w_out_a, w_out_b, ln_x, w_cq, cqn_s, kc, vc, w_co, tm):
    t_total = x2d.shape[0]
    nblk_seq = seq // tm
    tok = lambda w: pl.BlockSpec((tm, w), lambda i: (i, 0))
    const = lambda a: pl.BlockSpec(a.shape, lambda i: (0,) * a.ndim)
    memspec = pl.BlockSpec((1, N_MEM, X_WIDTH), lambda i: (i // nblk_seq, 0, 0))
    return pl.pallas_call(
        _mid_kernel,
        out_shape=jax.ShapeDtypeStruct((t_total, D_MODEL), F32),
        grid=(t_total // tm,),
        in_specs=[tok(D_MODEL), tok(Q_WIDTH), tok(Q_WIDTH), const(w_out_a), const(w_out_b), const(ln_x),
                  const(w_cq), const(cqn_s), memspec, memspec, const(w_co)],
        out_specs=tok(D_MODEL),
        compiler_params=_params("parallel"),
        name="mid",
    )(x2d, oa, ob, w_out_a, w_out_b, ln_x, w_cq, cqn_s, kc, vc, w_co)


def _oddeven_merge(lo, hi, r):
    step = r * 2
    if step < hi - lo:
        yield from _oddeven_merge(lo, hi, step)
        yield from _oddeven_merge(lo + r, hi, step)
        yield from [(i, i + r) for i in range(lo + r, hi - r, step)]
    else:
        yield (lo, lo + r)


def _oddeven_sort_pairs(lo, hi):
    if hi - lo >= 1:
        mid = lo + (hi - lo) // 2
        yield from _oddeven_sort_pairs(lo, mid)
        yield from _oddeven_sort_pairs(mid + 1, hi)
        yield from _oddeven_merge(lo, hi, 1)


_SORT16 = tuple(_oddeven_sort_pairs(0, PEER_TOPK - 1))
_BITONIC16 = tuple((i, i + d) for d in (8, 4, 2, 1) for i in range(PEER_TOPK) if not i & d)


def _apply_network(xs, pairs):
    xs = list(xs)
    for i, j in pairs:
        xs[i], xs[j] = jnp.maximum(xs[i], xs[j]), jnp.minimum(xs[i], xs[j])
    return xs


def _top_half(xs, ys):
    n = len(xs)
    return [jnp.maximum(xs[a], ys[n - 1 - a]) for a in range(n)]


def _top16_sorted(x):
    rows = [x[a * SUBLANES:(a + 1) * SUBLANES, :] for a in range(PEER_NKEYS // SUBLANES)]
    rows = _apply_network(rows, _SORT16)
    for shift in (4, 2, 1):
        other = [pltpu.roll(r, shift, 0) for r in rows]
        rows = _apply_network(_top_half(rows, other), _BITONIC16)
    return rows


def _top16_products(e1, e2):
    k = PEER_TOPK
    row = lambda r: [e1[r] * e2[q] for q in range(k // (r + 1))]
    a = row(0)
    col = [e1[r] * e2[0] for r in range(8, 16)]
    b = _apply_network(row(1) + col[::-1], _BITONIC16)
    c = _apply_network(row(2) + row(3) + row(4) + row(5) + row(6), _SORT16)
    d = row(7)
    t = _apply_network(_top_half(a, b), _BITONIC16)
    t = _apply_network(_top_half(t, c), _BITONIC16)
    t[k - 1] = jnp.maximum(t[k - 1], d[0])
    t[k - 2] = jnp.maximum(t[k - 2], d[1])
    return t


def _tree(op, xs):
    xs = list(xs)
    while len(xs) > 1:
        xs = [op(xs[i], xs[i + 1]) for i in range(0, len(xs) - 1, 2)] + ([xs[-1]] if len(xs) % 2 else [])
    return xs[0]


def _pprep_kernel(x_ref, g_ref, wpq_ref, pk1_ref, pk2_ref, ht_ref, e1_ref, e2_ref, thr_ref, p1_ref, p2_ref, rz_ref,
                  *, tm):
    hf = _rms(x_ref[...], g_ref[...])
    h = hf.astype(BF16)
    ht_ref[...] = hf.T.astype(BF16)
    half = PEER_QDIM // 2
    ngrp = tm // LANES

    def head(hd, carry):
        q = _dot(h, wpq_ref[hd]).astype(BF16)
        s1 = _dot_nt(pk1_ref[hd], q[:, :half])
        s2 = _dot_nt(pk2_ref[hd], q[:, half:])
        e1 = jnp.exp(s1 - jnp.max(s1, axis=0, keepdims=True))
        e2 = jnp.exp(s2 - jnp.max(s2, axis=0, keepdims=True))
        r0 = pl.multiple_of(hd * PEER_NKEYS, PEER_NKEYS)
        e1_ref[pl.ds(r0, PEER_NKEYS), :] = e1
        e2_ref[pl.ds(r0, PEER_NKEYS), :] = e2
        for lg in range(ngrp):
            ls = slice(lg * LANES, (lg + 1) * LANES)
            t1 = _top16_sorted(e1[:, ls])
            t2 = _top16_sorted(e2[:, ls])
            for r in range(PEER_TOPK):
                p1_ref[hd, r, :, ls] = t1[r]
                p2_ref[hd, r, :, ls] = t2[r]
        return carry

    lax.fori_loop(0, PEER_HEADS, head, 0)

    sub = lax.broadcasted_iota(jnp.int32, (SUBLANES, LANES), 0)

    def packed(ref, r, ls):
        out = ref[0, r, :, ls]
        for hd in range(1, PEER_HEADS):
            out = jnp.where(sub == hd, ref[hd, r, :, ls], out)
        return out

    for lg in range(ngrp):
        ls = slice(lg * LANES, (lg + 1) * LANES)
        t1 = [packed(p1_ref, r, ls) for r in range(PEER_TOPK)]
        t2 = [packed(p2_ref, r, ls) for r in range(PEER_TOPK)]
        rz = 0.5 / _tree(jnp.add, _top16_products(t1, t2))
        t1n = [a * rz for a in t1]
        thr_ref[:, ls] = _tree(jnp.minimum, _top16_products(t1n, t2))
        rz_ref[:, ls] = rz

    for hd in range(PEER_HEADS):
        rs = slice(hd * PEER_NKEYS, (hd + 1) * PEER_NKEYS)
        e1_ref[rs, :] = e1_ref[rs, :] * rz_ref[hd:hd + 1, :]


def _pprep(x2, ln_ff, wpq_h, pk1, pk2, tm):
    t_total = x2.shape[0]
    const = lambda a: pl.BlockSpec(a.shape, lambda i: (0,) * a.ndim)
    hk = PEER_HEADS * PEER_NKEYS
    return pl.pallas_call(
        functools.partial(_pprep_kernel, tm=tm),
        out_shape=[jax.ShapeDtypeStruct((D_MODEL, t_total), BF16),
                   jax.ShapeDtypeStruct((hk, t_total), F32),
                   jax.ShapeDtypeStruct((hk, t_total), F32),
                   jax.ShapeDtypeStruct((PEER_HEADS, t_total), F32)],
        grid=(t_total // tm,),
        in_specs=[pl.BlockSpec((tm, D_MODEL), lambda i: (i, 0)), const(ln_ff), const(wpq_h), const(pk1), const(pk2)],
        out_specs=[pl.BlockSpec((D_MODEL, tm), lambda i: (0, i)),
                   pl.BlockSpec((hk, tm), lambda i: (0, i)),
                   pl.BlockSpec((hk, tm), lambda i: (0, i)),
                   pl.BlockSpec((PEER_HEADS, tm), lambda i: (0, i))],
        scratch_shapes=[pltpu.VMEM((PEER_HEADS, PEER_TOPK, SUBLANES, tm), F32),
                        pltpu.VMEM((PEER_HEADS, PEER_TOPK, SUBLANES, tm), F32),
                        pltpu.VMEM((PEER_HEADS, tm), F32)],
        compiler_params=_params("parallel"),
        name="pprep",
    )(x2, ln_ff, wpq_h, pk1, pk2)


def _pdense_kernel(x_ref, ht_ref, e1_ref, e2_ref, thr_ref, u_ref, vt_ref, o_ref, acc_ref, a_ref, wg_ref, *, rows):
    e = pl.program_id(1)
    tm = acc_ref.shape[1]
    slab = 2 * SUBLANES

    @pl.when(e == 0)
    def _():
        acc_ref[...] = jnp.zeros_like(acc_ref)

    a_ref[...] = _dot(u_ref[...], ht_ref[...])
    for r in range(rows):
        for lg in range(tm // LANES):
            ls = slice(lg * LANES, (lg + 1) * LANES)
            e1b = [jnp.broadcast_to(e1_ref[hd, r:r + 1, ls], (slab, LANES)) for hd in range(PEER_HEADS)]
            thb = [jnp.broadcast_to(thr_ref[hd:hd + 1, ls], (slab, LANES)) for hd in range(PEER_HEADS)]
            for sg in range(PEER_NKEYS // slab):
                w = None
                for hd in range(PEER_HEADS):
                    p = e1b[hd] * e2_ref[hd * PEER_NKEYS + sg * slab:hd * PEER_NKEYS + (sg + 1) * slab, ls]
                    sel = jnp.where(p >= thb[hd], p, 0.0)
                    w = sel if w is None else w + sel
                ss = slice(r * PEER_NKEYS + sg * slab, r * PEER_NKEYS + (sg + 1) * slab)
                a = a_ref[ss, ls]
                act = a * (1.0 + lax.erf(a * (2.0 ** -0.5)))
                wg_ref[ss, ls] = (w * act).astype(BF16)
    acc_ref[...] += _dot(vt_ref[...], wg_ref[...])

    @pl.when(e == pl.num_programs(1) - 1)
    def _():
        o_ref[...] = x_ref[...] + acc_ref[...].T


def _pdense(x2, ht, e1n, e2, thr, u, vt, tm, rows):
    t_total = x2.shape[0]
    en = rows * PEER_NKEYS
    hk = PEER_HEADS * PEER_NKEYS
    assert rows % SUBLANES == 0
    e1n = e1n.reshape(PEER_HEADS, PEER_NKEYS, t_total)
    return pl.pallas_call(
        functools.partial(_pdense_kernel, rows=rows),
        out_shape=jax.ShapeDtypeStruct((t_total, D_MODEL), F32),
        grid=(t_total // tm, PEER_EXPERTS // en),
        in_specs=[pl.BlockSpec((tm, D_MODEL), lambda i, j: (i, 0)),
                  pl.BlockSpec((D_MODEL, tm), lambda i, j: (0, i)),
                  pl.BlockSpec((PEER_HEADS, rows, tm), lambda i, j: (0, j, i)),
                  pl.BlockSpec((hk, tm), lambda i, j: (0, i)),
                  pl.BlockSpec((PEER_HEADS, tm), lambda i, j: (0, i)),
                  pl.BlockSpec((en, D_MODEL), lambda i, j: (j, 0)),
                  pl.BlockSpec((D_MODEL, en), lambda i, j: (0, j))],
        out_specs=pl.BlockSpec((tm, D_MODEL), lambda i, j: (i, 0)),
        scratch_shapes=[pltpu.VMEM((D_MODEL, tm), F32), pltpu.VMEM((en, tm), F32), pltpu.VMEM((en, tm), BF16)],
        compiler_params=_params("parallel", "arbitrary"),
        name="pdense",
    )(x2, ht, e1n, e2, thr, u, vt)


def _prepare_weights(ln_mix, w_in, qn_a, kn_a, sink_a, qn_b, kn_b, go_a, go_b, w_out, ln_x, ln_mem, w_cq, w_ckv,
                     cqn, ckn, w_co, ln_ff, w_pq, pk1, pk2, peer_u, peer_v):
    row = lambda a: a.reshape(1, -1).astype(F32)
    qa0, ka0, va0 = 0, Q_WIDTH, Q_WIDTH + KV_WIDTH
    qb0 = Q_WIDTH + 2 * KV_WIDTH
    kb0, vb0 = qb0 + Q_WIDTH, qb0 + Q_WIDTH + KV_WIDTH
    cols = lambda s, w: w_in[:, s:s + w]
    w_in_p = jnp.concatenate([cols(qa0, Q_WIDTH), cols(qb0, Q_WIDTH), cols(ka0, KV_WIDTH), cols(kb0, KV_WIDTH),
                              cols(va0, KV_WIDTH), cols(vb0, KV_WIDTH)], axis=1).astype(BF16)
    scale = HEAD_DIM ** -0.5 * LOG2_E
    gain =jnp.concatenate([jnp.tile(qn_a, Q_HEADS) * scale, jnp.tile(qn_b, Q_HEADS) * scale,
                            jnp.tile(kn_a, KV_HEADS), jnp.tile(kn_b, KV_HEADS)]).reshape(1, NORM_COLS).astype(F32)
    lane = np.arange(LANES)
    seg = jnp.asarray((lane[:, None] // HEAD_DIM) == (lane[None, :] // HEAD_DIM), dtype=BF16)
    return dict(
        ln_mix=row(ln_mix), w_in_p=w_in_p, gain=gain, seg=seg,
        sink=jnp.repeat(sink_a.astype(F32) * LOG2_E, WINDOW).reshape(KV_HEADS, GROUP * WINDOW), go_a=row(go_a), go_b=row(go_b),
        w_out_a=w_out[:Q_WIDTH].astype(BF16), w_out_b=w_out[Q_WIDTH:].astype(BF16),
        ln_x=row(ln_x), ln_mem=row(ln_mem), w_cq=w_cq.astype(BF16), w_ckv=w_ckv.astype(BF16),
        cqn_s=row(cqn) * (X_HEAD_DIM ** -0.5), ckn=row(ckn), w_co=w_co.astype(BF16), ln_ff=row(ln_ff),
        wpq_h=w_pq.reshape(D_MODEL, PEER_HEADS, PEER_QDIM).transpose(1, 0, 2).astype(BF16),
        pk1=pk1.astype(BF16), pk2=pk2.astype(BF16), u=peer_u.astype(BF16), vt=peer_v.T.astype(BF16))


def _layer(x, mem, p, *, tm_in=512, tq_b=256, tk_b=512, tm_mid=512, tm_prep=512, tm_dense=512, rows=8):
    b, seq, _ = x.shape
    x2d = x.reshape(b * seq, D_MODEL)
    tables = _rope_tables(seq)
    qa, qb, ka, kb, va, vb = _inproj(x2d, seq, p["ln_mix"], p["w_in_p"], p["seg"], p["gain"], tables, tm_in)
    r3 = lambda a: a.reshape(b, seq, a.shape[-1])
    oa = _attn_a(qa, r3(ka), r3(va), p["sink"], p["go_a"])
    ob = _attn_b(qb, r3(kb), r3(vb), p["go_b"], tq_b, tk_b)
    kc, vc = _memkv(mem, p["ln_mem"], p["w_ckv"], p["ckn"])
    x2 = _mid(x2d, seq, oa.reshape(b * seq, Q_WIDTH), ob.reshape(b * seq, Q_WIDTH), p["w_out_a"], p["w_out_b"],
              p["ln_x"], p["w_cq"], p["cqn_s"], kc, vc, p["w_co"], tm_mid)
    h3, e1n, e2, thr = _pprep(x2, p["ln_ff"], p["wpq_h"], p["pk1"], p["pk2"], tm_prep)
    y = _pdense(x2, h3, e1n, e2, thr, p["u"], p["vt"], tm_dense, rows)
    return y.reshape(b, seq, D_MODEL)


def kernel(x_prompt, x_sample, mem_prompt, mem_sample, ln_mix, w_in, qn_a, kn_a, sink_a, qn_b, kn_b, go_a, go_b,
           w_out, ln_x, ln_mem, w_cq, w_ckv, cqn, ckn, w_co, ln_ff, w_pq, pk1, pk2, peer_u, peer_v):
    params = (ln_mix, w_in, qn_a, kn_a, sink_a, qn_b, kn_b, go_a, go_b, w_out, ln_x, ln_mem, w_cq, w_ckv,
              cqn, ckn, w_co, ln_ff, w_pq, pk1, pk2, peer_u, peer_v)
    y_prompt, y_sample = x_prompt, x_sample
    for l in range(ln_mix.shape[0]):
        p = _prepare_weights(*[a[l] for a in params])
        y_prompt = _layer(y_prompt, mem_prompt, p)
        y_sample = _layer(y_sample, mem_sample, p)
    return (y_prompt, y_sample)
```

```python
import functools

import numpy as np
import jax
import jax.numpy as jnp
from jax import lax
from jax.experimental import pallas as pl
from jax.experimental.pallas import tpu as pltpu

F32 = jnp.float32
BF16 = jnp.bfloat16

D_MODEL = 1024
HEAD_DIM = 64
Q_HEADS = 8
KV_HEADS = 2
GROUP = Q_HEADS // KV_HEADS
Q_WIDTH = Q_HEADS * HEAD_DIM
KV_WIDTH = KV_HEADS * HEAD_DIM
IN_COLS = 2 * Q_WIDTH + 4 * KV_WIDTH
NORM_COLS = 2 * Q_WIDTH + 2 * KV_WIDTH
WINDOW = 128
ROPE_THETA = 10000.0
GRID_W = 64
N_MEM = 256
X_HEADS = 4
X_HEAD_DIM = 128
X_WIDTH = X_HEADS * X_HEAD_DIM
PEER_HEADS = 8
PEER_NKEYS = 128
PEER_EXPERTS = PEER_NKEYS * PEER_NKEYS
PEER_QDIM = 256
PEER_TOPK = 16
EPS = 1e-6
NEG = -1e30
LOG2_E = 1.4426950408889634

LANES = 128
SUBLANES = 8
VMEM_LIMIT = 56 * 1024 * 1024


def _params(*sem):
    return pltpu.CompilerParams(dimension_semantics=sem, vmem_limit_bytes=VMEM_LIMIT)


def _rms(x, g):
    ms = jnp.mean(x * x, axis=-1, keepdims=True)
    return x * lax.rsqrt(ms + EPS) * g


def _dot(a, b):
    return jnp.dot(a, b, preferred_element_type=F32)


def _dot_nt(a, b):
    return lax.dot_general(a, b, (((1,), (1,)), ((), ())), preferred_element_type=F32)


def _inproj_kernel(x_ref, g_ref, w_ref, seg_ref, gain_ref, cosa_ref, sina_ref, cosb_ref, sinb_ref,
                   qa_ref, qb_ref, ka_ref, kb_ref, va_ref, vb_ref):
    h = _rms(x_ref[...], g_ref[...]).astype(BF16)
    z = _dot(h, w_ref[...])
    lane = lax.broadcasted_iota(jnp.int32, (1, LANES), 1)
    seg = seg_ref[...]

    def norm_rope(c, cos, sin, first, half):
        zc = z[:, c * LANES:(c + 1) * LANES]
        sq = zc * zc
        hi = sq.astype(BF16)
        lo = (sq - hi.astype(F32)).astype(BF16)
        ms = (_dot(hi, seg) + _dot(lo, seg)) * (1.0 / HEAD_DIM)
        y = zc * lax.rsqrt(ms + EPS) * gain_ref[:, c * LANES:(c + 1) * LANES]
        rot = jnp.where(first, pltpu.roll(y, LANES - half, 1), pltpu.roll(y, half, 1))
        return y * cos + rot * sin

    first_a = (lane % HEAD_DIM) < (HEAD_DIM // 2)
    first_b = (lane % (HEAD_DIM // 2)) < (HEAD_DIM // 4)
    cosa, sina, cosb, sinb = cosa_ref[...], sina_ref[...], cosb_ref[...], sinb_ref[...]
    nq = Q_WIDTH // LANES
    for c in range(nq):
        qa_ref[c * LANES:(c + 1) * LANES, :] = norm_rope(c, cosa, sina, first_a, HEAD_DIM // 2).T.astype(BF16)
        qb_ref[:, c * LANES:(c + 1) * LANES] = norm_rope(nq + c, cosb, sinb, first_b, HEAD_DIM // 4).astype(BF16)
    ka_ref[...] = norm_rope(2 * nq, cosa, sina, first_a, HEAD_DIM // 2).astype(BF16)
    kb_ref[...] = norm_rope(2 * nq + 1, cosb, sinb, first_b, HEAD_DIM // 4).astype(BF16)
    va_ref[...] = z[:, NORM_COLS:NORM_COLS + KV_WIDTH].astype(BF16)
    vb_ref[...] = z[:, NORM_COLS + KV_WIDTH:].astype(BF16)


def _rope_tables(seq):
    lane = np.arange(LANES)
    t = jnp.arange(seq, dtype=jnp.int32)
    inv_a = ROPE_THETA ** (-jnp.arange(0, HEAD_DIM, 2, dtype=F32) / HEAD_DIM)
    ang_a = t.astype(F32)[:, None] * inv_a[None, :]
    idx_a = lane % (HEAD_DIM // 2)
    sign_a = np.where((lane % HEAD_DIM) < HEAD_DIM // 2, -1.0, 1.0).astype(np.float32)
    cos_a = jnp.cos(ang_a)[:, idx_a]
    sin_a = jnp.sin(ang_a)[:, idx_a] * sign_a[None, :]
    hd2 = HEAD_DIM // 2
    inv_b = ROPE_THETA ** (-jnp.arange(0, hd2, 2, dtype=F32) / hd2)
    row = (t // GRID_W).astype(F32)
    col = (t % GRID_W).astype(F32)
    ang_row = row[:, None] * inv_b[None, :]
    ang_col = col[:, None] * inv_b[None, :]
    idx_b = lane % (hd2 // 2)
    is_row = ((lane % HEAD_DIM) < hd2)[None, :]
    sign_b = np.where((lane % hd2) < hd2 // 2, -1.0, 1.0).astype(np.float32)
    cos_b = jnp.where(is_row, jnp.cos(ang_row)[:, idx_b], jnp.cos(ang_col)[:, idx_b])
    sin_b = jnp.where(is_row, jnp.sin(ang_row)[:, idx_b], jnp.sin(ang_col)[:, idx_b]) * sign_b[None, :]
    return cos_a, sin_a, cos_b, sin_b


def _inproj(x2d, seq, ln_mix, w_in_p, seg, gain, tables, tm):
    t_total = x2d.shape[0]
    nblk_seq = seq // tm
    tok = lambda w: pl.BlockSpec((tm, w), lambda i: (i, 0))
    const = lambda a: pl.BlockSpec(a.shape, lambda i: (0,) * a.ndim)
    tab = pl.BlockSpec((tm, LANES), lambda i: (i % nblk_seq, 0))
    qt = pl.BlockSpec((Q_WIDTH, tm), lambda i: (0, i))
    outs = ([jax.ShapeDtypeStruct((Q_WIDTH, t_total), BF16), jax.ShapeDtypeStruct((t_total, Q_WIDTH), BF16)]
            + [jax.ShapeDtypeStruct((t_total, KV_WIDTH), BF16)] * 4)
    return pl.pallas_call(
        _inproj_kernel,
        out_shape=outs,
        grid=(t_total // tm,),
        in_specs=[tok(D_MODEL), const(ln_mix), const(w_in_p), const(seg), const(gain), tab, tab, tab, tab],
        out_specs=[qt, tok(Q_WIDTH), tok(KV_WIDTH), tok(KV_WIDTH), tok(KV_WIDTH), tok(KV_WIDTH)],
        compiler_params=_params("parallel"),
        name="inproj",
    )(x2d, ln_mix, w_in_p, seg, gain, *tables)


def _group_queries(qt_ref, g):
    heads = [qt_ref[(g * GROUP + j) * HEAD_DIM:(g * GROUP + j + 1) * HEAD_DIM, :] for j in range(GROUP)]
    qg = jnp.concatenate(heads, axis=1)
    parts = [qg if kv == g else jnp.zeros_like(qg) for kv in range(KV_HEADS)]
    return jnp.concatenate(parts, axis=0)


def _softmax_block(s_ref, p_ref, m_prev):
    rows, w = s_ref.shape
    slab = 2 * SUBLANES
    maxes, sums = [], []
    for lg in range(w // LANES):
        ls = slice(lg * LANES, (lg + 1) * LANES)
        mx = s_ref[0:SUBLANES, ls]
        for sl in range(1, rows // SUBLANES):
            mx = jnp.maximum(mx, s_ref[sl * SUBLANES:(sl + 1) * SUBLANES, ls])
        maxes.append(jnp.max(mx, axis=0, keepdims=True))
    m_new = jnp.maximum(m_prev, jnp.concatenate(maxes, axis=1))
    for lg in range(w // LANES):
        ls = slice(lg * LANES, (lg + 1) * LANES)
        mrow = jnp.broadcast_to(m_new[:, ls], (slab, LANES))
        tot = None
        for sl in range(rows // slab):
            p = jnp.exp2(s_ref[sl * slab:(sl + 1) * slab, ls] - mrow)
            tot = p if tot is None else tot + p
            p_ref[sl * slab:(sl + 1) * slab, ls] = p.astype(BF16)
        sums.append(jnp.sum(tot, axis=0, keepdims=True))
    return m_new, jnp.concatenate(sums, axis=1)


def _dot_tn(a, b):
    return lax.dot_general(a, b, (((0,), (0,)), ((), ())), preferred_element_type=F32)


def _finish_heads(ot_ref, g, og, tq):
    for j in range(GROUP):
        h = g * GROUP + j
        ot_ref[h * HEAD_DIM:(h + 1) * HEAD_DIM, :] = og[:, j * tq:(j + 1) * tq]


def _attn_a_kernel(qt_ref, k_ref, v_ref, sink_ref, go_ref, o_ref, s_ref, p_ref, ot_ref, *, seq):
    bq = WINDOW
    span = bq + 2 * WINDOW
    w = GROUP * bq
    start = pl.program_id(1) * bq
    c = pl.multiple_of(jnp.clip(start - WINDOW, 0, seq - span), bq)
    k = k_ref[0, pl.ds(c, span), :]
    v = v_ref[0, pl.ds(c, span), :]
    kpos = c + lax.broadcasted_iota(jnp.int32, (span, 1), 0)
    qpos = start + lax.broadcasted_iota(jnp.int32, (1, w), 1) % bq
    valid = jnp.abs(kpos - qpos) <= WINDOW
    for g in range(KV_HEADS):
        s_ref[...] = jnp.where(valid, _dot(k, _group_queries(qt_ref, g)), NEG)
        sink = sink_ref[g:g + 1, :]
        m, tot = _softmax_block(s_ref, p_ref, sink)
        den = tot + jnp.exp2(sink - m)
        pv = _dot_tn(v, p_ref[...])
        _finish_heads(ot_ref, g, pv[g * HEAD_DIM:(g + 1) * HEAD_DIM, :] / den, bq)
    o_ref[0] = _rms(ot_ref[...].T, go_ref[...]).astype(BF16)


def _attn_a(qt, k, v, sink_w, go):
    b, seq, _ = k.shape
    assert seq % WINDOW == 0 and seq >= 3 * WINDOW
    nq = seq // WINDOW
    span = 3 * WINDOW
    w = GROUP * WINDOW
    return pl.pallas_call(
        functools.partial(_attn_a_kernel, seq=seq),
        out_shape=jax.ShapeDtypeStruct((b, seq, Q_WIDTH), BF16),
        grid=(b, nq),
        in_specs=[pl.BlockSpec((Q_WIDTH, WINDOW), lambda i, j: (0, i * nq + j)),
                  pl.BlockSpec((1, seq, KV_WIDTH), lambda i, j: (i, 0, 0)),
                  pl.BlockSpec((1, seq, KV_WIDTH), lambda i, j: (i, 0, 0)),
                  pl.BlockSpec(sink_w.shape, lambda i, j: (0, 0)),
                  pl.BlockSpec(go.shape, lambda i, j: (0, 0))],
        out_specs=pl.BlockSpec((1, WINDOW, Q_WIDTH), lambda i, j: (i, j, 0)),
        scratch_shapes=[pltpu.VMEM((span, w), F32), pltpu.VMEM((span, w), BF16), pltpu.VMEM((Q_WIDTH, WINDOW), F32)],
        compiler_params=_params("parallel", "arbitrary"),
        name="attn_a",
    )(qt, k, v, sink_w, go)


def _attn_b_kernel(q_ref, k_ref, v_ref, go_ref, o_ref, s_ref, p_ref, m_ref, l_ref, alpha_ref, acc_ref,
                   *, seq, tq, tk):
    rows = GROUP * tq
    slab = 2 * SUBLANES
    nlg = tk // LANES
    lane = lax.broadcasted_iota(jnp.int32, (1, LANES), 1)
    pieces = {}
    for g in range(KV_HEADS):
        in_group = (lane // HEAD_DIM) == g
        heads = []
        for j in range(GROUP):
            h = g * GROUP + j
            c = (h * HEAD_DIM) // LANES
            chunk = q_ref[0, :, c * LANES:(c + 1) * LANES].astype(F32)
            if h % (LANES // HEAD_DIM) != g:
                chunk = pltpu.roll(chunk, HEAD_DIM, 1)
            heads.append(jnp.where(in_group, chunk, 0.0).astype(BF16))
        q4 = jnp.concatenate(heads, axis=0)
        m_ref[...] = jnp.full_like(m_ref, NEG)
        l_ref[...] = jnp.zeros_like(l_ref)
        acc_ref[...] = jnp.zeros_like(acc_ref)

        def body(kb, carry, q4=q4):
            off = pl.multiple_of(kb * tk, tk)
            s_ref[...] = _dot_nt(q4, k_ref[0, pl.ds(off, tk), :])
            for sl in range(rows // slab):
                rs = slice(sl * slab, (sl + 1) * slab)
                s = [s_ref[rs, lg * LANES:(lg + 1) * LANES] for lg in range(nlg)]
                m_old = m_ref[rs, :]
                m_new = jnp.maximum(m_old, jnp.max(_tree(jnp.maximum, s), axis=1, keepdims=True))
                alpha = jnp.exp2(m_old - m_new)
                p = [jnp.exp2(x - m_new) for x in s]
                for lg in range(nlg):
                    p_ref[rs, lg * LANES:(lg + 1) * LANES] = p[lg].astype(BF16)
                m_ref[rs, :] = m_new
                alpha_ref[rs, :] = alpha
                l_ref[rs, :] = alpha * l_ref[rs, :] + jnp.sum(_tree(jnp.add, p), axis=1, keepdims=True)
            pv = _dot(p_ref[...], v_ref[0, pl.ds(off, tk), :])
            acc_ref[...] = acc_ref[...] * alpha_ref[...] + pv
            return carry

        lax.fori_loop(0, seq // tk, body, 0)
        og = acc_ref[...] / l_ref[...]
        for j in range(GROUP):
            h = g * GROUP + j
            piece = og[j * tq:(j + 1) * tq, :]
            if h % (LANES // HEAD_DIM) != g:
                piece = pltpu.roll(piece, HEAD_DIM, 1)
            pieces[h] = piece
    per_chunk = LANES // HEAD_DIM
    chunks = [jnp.where(lane < HEAD_DIM, pieces[per_chunk * c], pieces[per_chunk * c + 1])
              for c in range(Q_WIDTH // LANES)]
    o_ref[0] = _rms(jnp.concatenate(chunks, axis=1), go_ref[...]).astype(BF16)


def _attn_b(q, k, v, go, tq, tk):
    b, seq, _ = q.shape
    rows = GROUP * tq
    stat = pltpu.VMEM((rows, LANES), F32)
    return pl.pallas_call(
        functools.partial(_attn_b_kernel, seq=seq, tq=tq, tk=tk),
        out_shape=jax.ShapeDtypeStruct((b, seq, Q_WIDTH), BF16),
        grid=(b, seq // tq),
        in_specs=[pl.BlockSpec((1, tq, Q_WIDTH), lambda i, j: (i, j, 0)),
                  pl.BlockSpec((1, seq, KV_WIDTH), lambda i, j: (i, 0, 0)),
                  pl.BlockSpec((1, seq, KV_WIDTH), lambda i, j: (i, 0, 0)),
                  pl.BlockSpec(go.shape, lambda i, j: (0, 0))],
        out_specs=pl.BlockSpec((1, tq, Q_WIDTH), lambda i, j: (i, j, 0)),
        scratch_shapes=[pltpu.VMEM((rows, tk), F32), pltpu.VMEM((rows, tk), BF16), stat, stat, stat, stat],
        compiler_params=_params("parallel", "arbitrary"),
        name="attn_b",
    )(q, k, v, go)


def _memkv_kernel(mem_ref, g_ref, w_ref, ckn_ref, k_ref, v_ref):
    h = _rms(mem_ref[0], g_ref[...]).astype(BF16)
    kv = _dot(h, w_ref[...])
    ckn = ckn_ref[...]
    for hd in range(X_HEADS):
        sl = slice(hd * X_HEAD_DIM, (hd + 1) * X_HEAD_DIM)
        k_ref[0, :, sl] = _rms(kv[:, sl], ckn).astype(BF16)
    v_ref[0] = kv[:, X_WIDTH:].astype(BF16)


def _memkv(mem, ln_mem, w_ckv, ckn):
    b = mem.shape[0]
    const = lambda a: pl.BlockSpec(a.shape, lambda i: (0,) * a.ndim)
    return pl.pallas_call(
        _memkv_kernel,
        out_shape=[jax.ShapeDtypeStruct((b, N_MEM, X_WIDTH), BF16)] * 2,
        grid=(b,),
        in_specs=[pl.BlockSpec((1, N_MEM, D_MODEL), lambda i: (i, 0, 0)), const(ln_mem), const(w_ckv), const(ckn)],
        out_specs=[pl.BlockSpec((1, N_MEM, X_WIDTH), lambda i: (i, 0, 0))] * 2,
        compiler_params=_params("parallel"),
        name="memkv",
    )(mem, ln_mem, w_ckv, ckn)


def _mid_kernel(x_ref, oa_ref, ob_ref, woa_ref, wob_ref, lnx_ref, wcq_ref, cqn_ref, kc_ref, vc_ref, wco_ref,
                o_ref):
    x1 = x_ref[...] + _dot(oa_ref[...], woa_ref[...]) + _dot(ob_ref[...], wob_ref[...])
    h = _rms(x1, lnx_ref[...]).astype(BF16)
    qc = _dot(h, wcq_ref[...])
    cqn = cqn_ref[...]
    outs = []
    for hd in range(X_HEADS):
        sl = slice(hd * X_HEAD_DIM, (hd + 1) * X_HEAD_DIM)
        qn = _rms(qc[:, sl], cqn).astype(BF16)
        s = _dot_nt(qn, kc_ref[0, :, sl])
        m = jnp.max(s, axis=-1, keepdims=True)
        p = jnp.exp(s - m)
        den = jnp.sum(p, axis=-1, keepdims=True)
        outs.append((_dot(p.astype(BF16), vc_ref[0, :, sl]) / den).astype(BF16))
    o = jnp.concatenate(outs, axis=-1)
    o_ref[...] = x1 + _dot(o, wco_ref[...])


def _mid(x2d, seq, oa, ob, w_out_a, w_out_b, ln_x, w_cq, cqn_s, kc, vc, w_co, tm):
    t_total = x2d.shape[0]
    nblk_seq = seq // tm
    tok = lambda w: pl.BlockSpec((tm, w), lambda i: (i, 0))
    const = lambda a: pl.BlockSpec(a.shape, lambda i: (0,) * a.ndim)
    memspec = pl.BlockSpec((1, N_MEM, X_WIDTH), lambda i: (i // nblk_seq, 0, 0))
    return pl.pallas_call(
        _mid_kernel,
        out_shape=jax.ShapeDtypeStruct((t_total, D_MODEL), F32),
        grid=(t_total // tm,),
        in_specs=[tok(D_MODEL), tok(Q_WIDTH), tok(Q_WIDTH), const(w_out_a), const(w_out_b), const(ln_x),
                  const(w_cq), const(cqn_s), memspec, memspec, const(w_co)],
        out_specs=tok(D_MODEL),
        compiler_params=_params("parallel"),
        name="mid",
    )(x2d, oa, ob, w_out_a, w_out_b, ln_x, w_cq, cqn_s, kc, vc, w_co)


def _oddeven_merge(lo, hi, r):
    step = r * 2
    if step < hi - lo:
        yield from _oddeven_merge(lo, hi, step)
        yield from _oddeven_merge(lo + r, hi, step)
        yield from [(i, i + r) for i in range(lo + r, hi - r, step)]
    else:
        yield (lo, lo + r)


def _oddeven_sort_pairs(lo, hi):
    if hi - lo >= 1:
        mid = lo + (hi - lo) // 2
        yield from _oddeven_sort_pairs(lo, mid)
        yield from _oddeven_sort_pairs(mid + 1, hi)
        yield from _oddeven_merge(lo, hi, 1)


_SORT16 = tuple(_oddeven_sort_pairs(0, PEER_TOPK - 1))
_BITONIC16 = tuple((i, i + d) for d in (8, 4, 2, 1) for i in range(PEER_TOPK) if not i & d)


def _apply_network(xs, pairs):
    xs = list(xs)
    for i, j in pairs:
        xs[i], xs[j] = jnp.maximum(xs[i], xs[j]), jnp.minimum(xs[i], xs[j])
    return xs


def _top_half(xs, ys):
    n = len(xs)
    return [jnp.maximum(xs[a], ys[n - 1 - a]) for a in range(n)]


def _top16_sorted(x):
    rows = [x[a * SUBLANES:(a + 1) * SUBLANES, :] for a in range(PEER_NKEYS // SUBLANES)]
    rows = _apply_network(rows, _SORT16)
    for shift in (4, 2, 1):
        other = [pltpu.roll(r, shift, 0) for r in rows]
        rows = _apply_network(_top_half(rows, other), _BITONIC16)
    return rows


def _top16_products(e1, e2, rnd=lambda x: x):
    k = PEER_TOPK
    row = lambda r: [rnd(e1[r] * e2[q]) for q in range(k // (r + 1))]
    a = row(0)
    col = [rnd(e1[r] * e2[0]) for r in range(8, 16)]
    b = _apply_network(row(1) + col[::-1], _BITONIC16)
    c = _apply_network(row(2) + row(3) + row(4) + row(5) + row(6), _SORT16)
    d = row(7)
    t = _apply_network(_top_half(a, b), _BITONIC16)
    t = _apply_network(_top_half(t, c), _BITONIC16)
    t[k - 1] = jnp.maximum(t[k - 1], d[0])
    t[k - 2] = jnp.maximum(t[k - 2], d[1])
    return t


def _tree(op, xs):
    xs = list(xs)
    while len(xs) > 1:
        xs = [op(xs[i], xs[i + 1]) for i in range(0, len(xs) - 1, 2)] + ([xs[-1]] if len(xs) % 2 else [])
    return xs[0]


def _pprep_kernel(x_ref, g_ref, wpq_ref, pk1_ref, pk2_ref, ht_ref, e1_ref, e2_ref, thr_ref, p1_ref, p2_ref, rz_ref,
                  *, tm):
    hf = _rms(x_ref[...], g_ref[...])
    h = hf.astype(BF16)
    ht_ref[...] = hf.T.astype(BF16)
    half = PEER_QDIM // 2
    ngrp = tm // LANES

    def head(hd, carry):
        q = _dot(h, wpq_ref[hd]).astype(BF16)
        s1 = _dot_nt(pk1_ref[hd], q[:, :half])
        s2 = _dot_nt(pk2_ref[hd], q[:, half:])
        e1 = jnp.exp(s1 - jnp.max(s1, axis=0, keepdims=True))
        e2 = jnp.exp(s2 - jnp.max(s2, axis=0, keepdims=True))
        r0 = pl.multiple_of(hd * PEER_NKEYS, PEER_NKEYS)
        e1_ref[pl.ds(r0, PEER_NKEYS), :] = e1
        e2_ref[pl.ds(r0, PEER_NKEYS), :] = e2.astype(BF16)
        for lg in range(ngrp):
            ls = slice(lg * LANES, (lg + 1) * LANES)
            t1 = _top16_sorted(e1[:, ls])
            t2 = _top16_sorted(e2[:, ls])
            for r in range(PEER_TOPK):
                p1_ref[hd, r, :, ls] = t1[r]
                p2_ref[hd, r, :, ls] = t2[r]
        return carry

    lax.fori_loop(0, PEER_HEADS, head, 0)

    sub = lax.broadcasted_iota(jnp.int32, (SUBLANES, LANES), 0)

    def packed(ref, r, ls):
        out = ref[0, r, :, ls]
        for hd in range(1, PEER_HEADS):
            out = jnp.where(sub == hd, ref[hd, r, :, ls], out)
        return out

    for lg in range(ngrp):
        ls = slice(lg * LANES, (lg + 1) * LANES)
        t1 = [packed(p1_ref, r, ls) for r in range(PEER_TOPK)]
        t2 = [packed(p2_ref, r, ls) for r in range(PEER_TOPK)]
        rz = 0.5 / _tree(jnp.add, _top16_products(t1, t2))
        rnd = lambda a: a.astype(BF16).astype(F32)
        t1n = [rnd(a * rz) for a in t1]
        t2n = [rnd(a) for a in t2]
        thr_ref[:, ls] = _tree(jnp.minimum, _top16_products(t1n, t2n, rnd))
        rz_ref[:, ls] = rz

    for hd in range(PEER_HEADS):
        rs = slice(hd * PEER_NKEYS, (hd + 1) * PEER_NKEYS)
        e1_ref[rs, :] = e1_ref[rs, :] * rz_ref[hd:hd + 1, :]


def _pprep(x2, ln_ff, wpq_h, pk1, pk2, tm):
    t_total = x2.shape[0]
    const = lambda a: pl.BlockSpec(a.shape, lambda i: (0,) * a.ndim)
    hk = PEER_HEADS * PEER_NKEYS
    return pl.pallas_call(
        functools.partial(_pprep_kernel, tm=tm),
        out_shape=[jax.ShapeDtypeStruct((D_MODEL, t_total), BF16),
                   jax.ShapeDtypeStruct((hk, t_total), F32),
                   jax.ShapeDtypeStruct((hk, t_total), BF16),
                   jax.ShapeDtypeStruct((PEER_HEADS, t_total), F32)],
        grid=(t_total // tm,),
        in_specs=[pl.BlockSpec((tm, D_MODEL), lambda i: (i, 0)), const(ln_ff), const(wpq_h), const(pk1), const(pk2)],
        out_specs=[pl.BlockSpec((D_MODEL, tm), lambda i: (0, i)),
                   pl.BlockSpec((hk, tm), lambda i: (0, i)),
                   pl.BlockSpec((hk, tm), lambda i: (0, i)),
                   pl.BlockSpec((PEER_HEADS, tm), lambda i: (0, i))],
        scratch_shapes=[pltpu.VMEM((PEER_HEADS, PEER_TOPK, SUBLANES, tm), F32),
                        pltpu.VMEM((PEER_HEADS, PEER_TOPK, SUBLANES, tm), F32),
                        pltpu.VMEM((PEER_HEADS, tm), F32)],
        compiler_params=_params("parallel"),
        name="pprep",
    )(x2, ln_ff, wpq_h, pk1, pk2)


def _pdense_kernel(x_ref, ht_ref, e1_ref, e2_ref, thr_ref, u_ref, vt_ref, o_ref, acc_ref, a_ref, wg_ref, *, rows):
    e = pl.program_id(1)
    tm = acc_ref.shape[1]
    slab = 2 * SUBLANES

    @pl.when(e == 0)
    def _():
        acc_ref[...] = jnp.zeros_like(acc_ref)

    a_ref[...] = _dot(u_ref[...], ht_ref[...])
    for r in range(rows):
        for lg in range(tm // LANES):
            ls = slice(lg * LANES, (lg + 1) * LANES)
            bcast = lambda row: jnp.broadcast_to(row, (slab, LANES)).astype(BF16)
            e1b = [bcast(e1_ref[hd, r:r + 1, ls]) for hd in range(PEER_HEADS)]
            thb = [bcast(thr_ref[hd:hd + 1, ls]) for hd in range(PEER_HEADS)]
            for sg in range(PEER_NKEYS // slab):
                w = None
                for hd in range(PEER_HEADS):
                    p = e1b[hd] * e2_ref[hd * PEER_NKEYS + sg * slab:hd * PEER_NKEYS + (sg + 1) * slab, ls]
                    sel = jnp.where(p >= thb[hd], p, jnp.zeros_like(p))
                    w = sel if w is None else w + sel
                ss = slice(r * PEER_NKEYS + sg * slab, r * PEER_NKEYS + (sg + 1) * slab)
                a = a_ref[ss, ls]
                act = a * (1.0 + lax.erf(a * (2.0 ** -0.5)))
                wg_ref[ss, ls] = w * act.astype(BF16)
    acc_ref[...] += _dot(vt_ref[...], wg_ref[...])

    @pl.when(e == pl.num_programs(1) - 1)
    def _():
        o_ref[...] = x_ref[...] + acc_ref[...].T


def _pdense(x2, ht, e1n, e2, thr, u, vt, tm, rows):
    t_total = x2.shape[0]
    en = rows * PEER_NKEYS
    hk = PEER_HEADS * PEER_NKEYS
    assert rows % SUBLANES == 0
    e1n = e1n.reshape(PEER_HEADS, PEER_NKEYS, t_total)
    return pl.pallas_call(
        functools.partial(_pdense_kernel, rows=rows),
        out_shape=jax.ShapeDtypeStruct((t_total, D_MODEL), F32),
        grid=(t_total // tm, PEER_EXPERTS // en),
        in_specs=[pl.BlockSpec((tm, D_MODEL), lambda i, j: (i, 0)),
                  pl.BlockSpec((D_MODEL, tm), lambda i, j: (0, i)),
                  pl.BlockSpec((PEER_HEADS, rows, tm), lambda i, j: (0, j, i)),
                  pl.BlockSpec((hk, tm), lambda i, j: (0, i)),
                  pl.BlockSpec((PEER_HEADS, tm), lambda i, j: (0, i)),
                  pl.BlockSpec((en, D_MODEL), lambda i, j: (j, 0)),
                  pl.BlockSpec((D_MODEL, en), lambda i, j: (0, j))],
        out_specs=pl.BlockSpec((tm, D_MODEL), lambda i, j: (i, 0)),
        scratch_shapes=[pltpu.VMEM((D_MODEL, tm), F32), pltpu.VMEM((en, tm), F32), pltpu.VMEM((en, tm), BF16)],
        compiler_params=_params("parallel", "arbitrary"),
        name="pdense",
    )(x2, ht, e1n, e2, thr, u, vt)


def _prepare_weights(ln_mix, w_in, qn_a, kn_a, sink_a, qn_b, kn_b, go_a, go_b, w_out, ln_x, ln_mem, w_cq, w_ckv,
                     cqn, ckn, w_co, ln_ff, w_pq, pk1, pk2, peer_u, peer_v):
    row = lambda a: a.reshape(1, -1).astype(F32)
    qa0, ka0, va0 = 0, Q_WIDTH, Q_WIDTH + KV_WIDTH
    qb0 = Q_WIDTH + 2 * KV_WIDTH
    kb0, vb0 = qb0 + Q_WIDTH, qb0 + Q_WIDTH + KV_WIDTH
    cols = lambda s, w: w_in[:, s:s + w]
    w_in_p = jnp.concatenate([cols(qa0, Q_WIDTH), cols(qb0, Q_WIDTH), cols(ka0, KV_WIDTH), cols(kb0, KV_WIDTH),
                              cols(va0, KV_WIDTH), cols(vb0, KV_WIDTH)], axis=1).astype(BF16)
    scale = HEAD_DIM ** -0.5 * LOG2_E
    gain =jnp.concatenate([jnp.tile(qn_a, Q_HEADS) * scale, jnp.tile(qn_b, Q_HEADS) * scale,
                            jnp.tile(kn_a, KV_HEADS), jnp.tile(kn_b, KV_HEADS)]).reshape(1, NORM_COLS).astype(F32)
    lane = np.arange(LANES)
    seg = jnp.asarray((lane[:, None] // HEAD_DIM) == (lane[None, :] // HEAD_DIM), dtype=BF16)
    return dict(
        ln_mix=row(ln_mix), w_in_p=w_in_p, gain=gain, seg=seg,
        sink=jnp.repeat(sink_a.astype(F32) * LOG2_E, WINDOW).reshape(KV_HEADS, GROUP * WINDOW), go_a=row(go_a), go_b=row(go_b),
        w_out_a=w_out[:Q_WIDTH].astype(BF16), w_out_b=w_out[Q_WIDTH:].astype(BF16),
        ln_x=row(ln_x), ln_mem=row(ln_mem), w_cq=w_cq.astype(BF16), w_ckv=w_ckv.astype(BF16),
        cqn_s=row(cqn) * (X_HEAD_DIM ** -0.5), ckn=row(ckn), w_co=w_co.astype(BF16), ln_ff=row(ln_ff),
        wpq_h=w_pq.reshape(D_MODEL, PEER_HEADS, PEER_QDIM).transpose(1, 0, 2).astype(BF16),
        pk1=pk1.astype(BF16), pk2=pk2.astype(BF16), u=peer_u.astype(BF16), vt=peer_v.T.astype(BF16))


def _layer(x, mem, p, *, tm_in=512, tq_b=256, tk_b=512, tm_mid=512, tm_prep=512, tm_dense=512, rows=8):
    b, seq, _ = x.shape
    x2d = x.reshape(b * seq, D_MODEL)
    tables = _rope_tables(seq)
    qa, qb, ka, kb, va, vb = _inproj(x2d, seq, p["ln_mix"], p["w_in_p"], p["seg"], p["gain"], tables, tm_in)
    r3 = lambda a: a.reshape(b, seq, a.shape[-1])
    oa = _attn_a(qa, r3(ka), r3(va), p["sink"], p["go_a"])
    ob = _attn_b(r3(qb), r3(kb), r3(vb), p["go_b"], tq_b, tk_b)
    kc, vc = _memkv(mem, p["ln_mem"], p["w_ckv"], p["ckn"])
    x2 = _mid(x2d, seq, oa.reshape(b * seq, Q_WIDTH), ob.reshape(b * seq, Q_WIDTH), p["w_out_a"], p["w_out_b"],
              p["ln_x"], p["w_cq"], p["cqn_s"], kc, vc, p["w_co"], tm_mid)
    h3, e1n, e2, thr = _pprep(x2, p["ln_ff"], p["wpq_h"], p["pk1"], p["pk2"], tm_prep)
    y = _pdense(x2, h3, e1n, e2, thr, p["u"], p["vt"], tm_dense, rows)
    return y.reshape(b, seq, D_MODEL)


def kernel(x_prompt, x_sample, mem_prompt, mem_sample, ln_mix, w_in, qn_a, kn_a, sink_a, qn_b, kn_b, go_a, go_b,
           w_out, ln_x, ln_mem, w_cq, w_ckv, cqn, ckn, w_co, ln_ff, w_pq, pk1, pk2, peer_u, peer_v):
    params = (ln_mix, w_in, qn_a, kn_a, sink_a, qn_b, kn_b, go_a, go_b, w_out, ln_x, ln_mem, w_cq, w_ckv,
              cqn, ckn, w_co, ln_ff, w_pq, pk1, pk2, peer_u, peer_v)
    y_prompt, y_sample = x_prompt, x_sample
    for l in range(ln_mix.shape[0]):
        p = _prepare_weights(*[a[l] for a in params])
        y_prompt = _layer(y_prompt, mem_prompt, p)
        y_sample = _layer(y_sample, mem_sample, p)
    return (y_prompt, y_sample)
```
